```python
import jax, jax.numpy as jnp
from jax import lax
import numpy as np

D_MODEL = 1024
BATCH = 4
SEQ = 8192
DEPTH = 1

CHUNK = 64
Q_BLOCK = 128
HEAD_DIM = 64
N_ATTN_HEADS = 8
D_ATTN = N_ATTN_HEADS * HEAD_DIM
N_IDX_HEADS = 8
IDX_DIM = 64
TOPK_MAX = 256
D_CONV = D_MODEL - D_ATTN
N_CONV_GROUPS = 8
CONV_WIDTH = 3
D_FF = 2816
D_PLE = 256
LN_EPS = 1e-5
DN_ALPHA = (2 * DEPTH) ** 0.25
DN_BETA = (8 * DEPTH) ** -0.25
IDX_SCALE = (N_IDX_HEADS ** -0.5) * (IDX_DIM ** -0.5)
SPLIT_SIZES = (D_ATTN, D_ATTN, D_ATTN, N_IDX_HEADS * IDX_DIM, IDX_DIM, N_IDX_HEADS, D_CONV, D_CONV, D_CONV)
D_IN = sum(SPLIT_SIZES)

kernel_name = "hybrid_dsa_shortconv_macaron_deepnorm"


def alibi_slopes():
    h = jnp.arange(1, N_ATTN_HEADS + 1, dtype=jnp.float32)
    return jnp.exp2(-8.0 * h / N_ATTN_HEADS)


def layer_norm(x, g, b):
    xf = x.astype(jnp.float32)
    mu = jnp.mean(xf, axis=-1, keepdims=True)
    xc = xf - mu
    var = jnp.mean(xc * xc, axis=-1, keepdims=True)
    y = xc * lax.rsqrt(var + LN_EPS) * g.astype(jnp.float32) + b.astype(jnp.float32)
    return y.astype(x.dtype)


def swiglu(x, wg, wu, wd):
    return (jax.nn.silu(x @ wg) * (x @ wu)) @ wd


def short_conv_mixer(bg, cg, u, conv_w):
    s = u.shape[1]
    z = cg * u
    zp = jnp.pad(z, ((0, 0), (CONV_WIDTH - 1, 0), (0, 0)))
    y = sum(conv_w[j] * zp[:, j:j + s] for j in range(CONV_WIDTH))
    return bg * y


def dsa_attention(q, k, v, q_idx, k_idx, w_idx):
    b, s = q.shape[0], q.shape[1]
    topk = min(TOPK_MAX, s // 4)
    nb = s // Q_BLOCK
    slopes = alibi_slopes()
    key_pos = jnp.arange(s)
    k_idx_f = k_idx.astype(jnp.float32)

    def block(args):
        qb, qib, wb, start = args
        q_pos = start + jnp.arange(Q_BLOCK)
        limit = (q_pos // CHUNK + 1) * CHUNK
        rel = jnp.einsum('bqhd,bsd->bqhs', qib.astype(jnp.float32), k_idx_f)
        score = jnp.einsum('bqhs,bqh->bqs', jax.nn.relu(rel), wb.astype(jnp.float32))
        admissible = key_pos[None, :] < limit[:, None]
        score = jnp.where(admissible[None], score, -jnp.inf)
        _, idx = lax.top_k(score, topk)
        valid = idx < limit[None, :, None]
        kg = jax.vmap(lambda kb, ib: kb[ib])(k, idx)
        vg = jax.vmap(lambda vb, ib: vb[ib])(v, idx)
        logits = jnp.einsum('bqhd,bqkhd->bhqk', qb, kg).astype(jnp.float32) * (HEAD_DIM ** -0.5)
        dist = jnp.abs(q_pos[None, :, None] - idx).astype(jnp.float32)
        logits = logits - slopes[None, :, None, None] * dist[:, None]
        logits = jnp.where(valid[:, None], logits, -jnp.inf)
        probs = jax.nn.softmax(logits, axis=-1).astype(v.dtype)
        return jnp.einsum('bhqk,bqkhd->bqhd', probs, vg)

    def to_blocks(a):
        return a.reshape(b, nb, Q_BLOCK, *a.shape[2:]).swapaxes(0, 1)

    starts = jnp.arange(nb) * Q_BLOCK
    out = lax.map(block, (to_blocks(q), to_blocks(q_idx), to_blocks(w_idx), starts))
    return out.swapaxes(0, 1).reshape(b, s, N_ATTN_HEADS * HEAD_DIM)


def setup_inputs(seed: int = 0) -> dict:
    key = jax.random.key(seed)
    ks = jax.random.split(key, 24)
    f32 = jnp.float32

    def nrm(k, shape, scale):
        return jax.random.normal(k, shape, f32) * scale

    x = nrm(ks[0], (BATCH, SEQ, D_MODEL), 1.0)
    p = nrm(ks[1], (DEPTH, BATCH, SEQ, D_PLE), 1.0)
    w_in = nrm(ks[2], (DEPTH, D_MODEL, D_IN), D_MODEL ** -0.5)
    w_in = w_in.at[:, :, 2 * D_ATTN:3 * D_ATTN].multiply(DN_BETA)
    return {
        "x": x,
        "p": p,
        "ln1_g": 1.0 + nrm(ks[3], (DEPTH, D_MODEL), 0.02),
        "ln1_b": nrm(ks[4], (DEPTH, D_MODEL), 0.02),
        "ffn1_wg": nrm(ks[5], (DEPTH, D_MODEL, D_FF), D_MODEL ** -0.5),
        "ffn1_wu": nrm(ks[6], (DEPTH, D_MODEL, D_FF), DN_BETA * D_MODEL ** -0.5),
        "ffn1_wd": nrm(ks[7], (DEPTH, D_FF, D_MODEL), DN_BETA * D_FF ** -0.5),
        "w_in": w_in,
        "conv_w": nrm(ks[8], (DEPTH, CONV_WIDTH, D_CONV), CONV_WIDTH ** -0.5),
        "w_out": nrm(ks[9], (DEPTH, D_MODEL, D_MODEL), DN_BETA * D_MODEL ** -0.5),
        "ln2_g": 1.0 + nrm(ks[10], (DEPTH, D_MODEL), 0.02),
        "ln2_b": nrm(ks[11], (DEPTH, D_MODEL), 0.02),
        "ffn2_wg": nrm(ks[12], (DEPTH, D_MODEL, D_FF), D_MODEL ** -0.5),
        "ffn2_wu": nrm(ks[13], (DEPTH, D_MODEL, D_FF), DN_BETA * D_MODEL ** -0.5),
        "ffn2_wd": nrm(ks[14], (DEPTH, D_FF, D_MODEL), DN_BETA * D_FF ** -0.5),
        "ln3_g": 1.0 + nrm(ks[15], (DEPTH, D_MODEL), 0.02),
        "ln3_b": nrm(ks[16], (DEPTH, D_MODEL), 0.02),
        "ple_gate_w": nrm(ks[17], (DEPTH, D_MODEL, D_MODEL), D_MODEL ** -0.5),
        "ple_proj_w": nrm(ks[18], (DEPTH, D_PLE, D_MODEL), D_PLE ** -0.5),
    }


def reference(x, p, ln1_g, ln1_b, ffn1_wg, ffn1_wu, ffn1_wd, w_in, conv_w, w_out,
              ln2_g, ln2_b, ffn2_wg, ffn2_wu, ffn2_wd, ln3_g, ln3_b, ple_gate_w, ple_proj_w):
    b, s, _ = x.shape
    split_points = list(np.cumsum(SPLIT_SIZES)[:-1])
    for i in range(DEPTH):
        x = layer_norm(DN_ALPHA * x + 0.5 * swiglu(x, ffn1_wg[i], ffn1_wu[i], ffn1_wd[i]), ln1_g[i], ln1_b[i])
        h = x @ w_in[i]
        q, k, v, qi, ki, wi, bg, cg, u = jnp.split(h, split_points, axis=-1)
        attn = dsa_attention(
            q.reshape(b, s, N_ATTN_HEADS, HEAD_DIM),
            k.reshape(b, s, N_ATTN_HEADS, HEAD_DIM),
            v.reshape(b, s, N_ATTN_HEADS, HEAD_DIM),
            qi.reshape(b, s, N_IDX_HEADS, IDX_DIM),
            ki,
            wi * IDX_SCALE,
        )
        conv = short_conv_mixer(bg, cg, u, conv_w[i])
        mix = jnp.concatenate([attn, conv], axis=-1) @ w_out[i]
        x = layer_norm(DN_ALPHA * x + mix, ln2_g[i], ln2_b[i])
        x = layer_norm(DN_ALPHA * x + 0.5 * swiglu(x, ffn2_wg[i], ffn2_wu[i], ffn2_wd[i]), ln3_g[i], ln3_b[i])
        x = x + jax.nn.sigmoid(x @ ple_gate_w[i]) * (p[i] @ ple_proj_w[i])
    return x
```

```python
import functools

import jax
import jax.numpy as jnp
from jax import lax
from jax.experimental import pallas as pl
from jax.experimental.pallas import tpu as pltpu

D_MODEL = 1024
CHUNK = 64
HEAD_DIM = 64
N_HEADS = 8
D_ATTN = N_HEADS * HEAD_DIM
N_IDX_HEADS = 8
IDX_DIM = 64
TOPK_MAX = 256
D_CONV = D_MODEL - D_ATTN
CONV_WIDTH = 3
D_FF = 2816
D_PLE = 256
LN_EPS = 1e-5
IDX_SCALE = (N_IDX_HEADS ** -0.5) * (IDX_DIM ** -0.5)

MXU_DTYPE = jnp.bfloat16

LANES = 128
FF_CHUNK = 256
ROW_TILE = 512
SEQ_TILE = 256
HEAD_PAD = 2 * HEAD_DIM
BISECT_ROUNDS = 16
VMEM_LIMIT_BYTES = 56 * 1024 * 1024


def _layer_norm(y, g, b):
    mu = jnp.mean(y, axis=-1, keepdims=True)
    yc = y - mu
    var = jnp.mean(yc * yc, axis=-1, keepdims=True)
    return yc * lax.rsqrt(var + LN_EPS) * g + b


def _swiglu(xb, wg_ref, wu_ref, wd_ref):
    n_chunks = wg_ref.shape[0]

    def body(c, acc):
        g = jnp.dot(xb, wg_ref[c], preferred_element_type=jnp.float32)
        u = jnp.dot(xb, wu_ref[c], preferred_element_type=jnp.float32)
        a = (g * jax.nn.sigmoid(g)) * u
        return acc + jnp.dot(a.astype(MXU_DTYPE), wd_ref[c], preferred_element_type=jnp.float32)

    return lax.fori_loop(0, n_chunks, body, jnp.zeros((xb.shape[0], wd_ref.shape[2]), jnp.float32))


def _ffn_ln_kernel(alpha, x_ref, wg_ref, wu_ref, wd_ref, g_ref, b_ref, o_ref):
    x = x_ref[...]
    ffn = _swiglu(x.astype(MXU_DTYPE), wg_ref, wu_ref, wd_ref)
    o_ref[...] = _layer_norm(alpha * x + 0.5 * ffn, g_ref[...], b_ref[...])


def _resident(shape):
    nd = len(shape)
    return pl.BlockSpec(shape, lambda *_: (0,) * nd, pipeline_mode=pl.Buffered(1))


def _ffn_ln(x2d, wg3, wu3, wd3, g, b, alpha):
    n, d = x2d.shape
    tm = min(ROW_TILE, n)
    return pl.pallas_call(
        functools.partial(_ffn_ln_kernel, alpha),
        grid=(n // tm,),
        in_specs=[
            pl.BlockSpec((tm, d), lambda i: (i, 0)),
            _resident(wg3.shape), _resident(wu3.shape), _resident(wd3.shape),
            _resident(g.shape), _resident(b.shape),
        ],
        out_specs=pl.BlockSpec((tm, d), lambda i: (i, 0)),
        out_shape=jax.ShapeDtypeStruct((n, d), jnp.float32),
        compiler_params=pltpu.CompilerParams(
            dimension_semantics=("arbitrary",), vmem_limit_bytes=VMEM_LIMIT_BYTES),
        name="ffn_ln",
    )(x2d, wg3, wu3, wd3, g, b)


def _in_proj_kernel(x_ref, wk_ref, wki_ref, wqt_ref, wvt_ref, wqit_ref, wwit_ref, wc_ref, cw_ref,
                    k_ref, kidx_ref, qt_ref, vt_ref, qit_ref, wit_ref, conv_ref, zbuf_ref):
    j = pl.program_id(1)
    tm = x_ref.shape[1]
    xb = x_ref[0].astype(MXU_DTYPE)
    nt = (((1,), (1,)), ((), ()))

    k_ref[0] = jnp.dot(xb, wk_ref[...], preferred_element_type=jnp.float32).astype(k_ref.dtype)
    kidx_ref[0] = jnp.dot(xb, wki_ref[...], preferred_element_type=jnp.float32).astype(kidx_ref.dtype)
    qt_ref[0, 0] = lax.dot_general(wqt_ref[...], xb, nt,
                                   preferred_element_type=jnp.float32).astype(qt_ref.dtype)
    vt_ref[0, 0] = lax.dot_general(wvt_ref[...], xb, nt,
                                   preferred_element_type=jnp.float32).astype(vt_ref.dtype)
    qit_ref[0, 0] = lax.dot_general(wqit_ref[...], xb, nt,
                                    preferred_element_type=jnp.float32).astype(qit_ref.dtype)
    wit = lax.dot_general(wwit_ref[...], xb, nt, preferred_element_type=jnp.float32)
    wit_ref[0, 0] = wit[0:N_IDX_HEADS, :] * IDX_SCALE

    h = jnp.dot(xb, wc_ref[...], preferred_element_type=jnp.float32)
    bg = h[:, 0:D_CONV]
    z = h[:, D_CONV:2 * D_CONV] * h[:, 2 * D_CONV:3 * D_CONV]

    @pl.when(j == 0)
    def _():
        zbuf_ref[0:8, :] = jnp.zeros((8, D_CONV), jnp.float32)

    zbuf_ref[8:8 + tm, :] = z
    z1 = zbuf_ref[7:7 + tm, :]
    z2 = zbuf_ref[6:6 + tm, :]
    y = (cw_ref[0:1, :] * z2 + cw_ref[1:2, :] * z1) + cw_ref[2:3, :] * z
    conv_ref[0] = (bg * y).astype(conv_ref.dtype)
    zbuf_ref[0:8, :] = z[tm - 8:tm, :]


def _in_proj(x1, wk, wki, wqt, wvt, wqit, wwit, wc, cw):
    b, s, d = x1.shape
    tm = SEQ_TILE
    ns = s // tm
    f32 = jnp.float32
    out_shape = (
        jax.ShapeDtypeStruct((b, s, D_ATTN), MXU_DTYPE),
        jax.ShapeDtypeStruct((b, s, LANES), MXU_DTYPE),
        jax.ShapeDtypeStruct((b, ns, N_HEADS * HEAD_PAD, tm), MXU_DTYPE),
        jax.ShapeDtypeStruct((b, ns, D_ATTN, tm), MXU_DTYPE),
        jax.ShapeDtypeStruct((b, ns, N_IDX_HEADS * HEAD_PAD, tm), MXU_DTYPE),
        jax.ShapeDtypeStruct((b, ns, N_IDX_HEADS, tm), f32),
        jax.ShapeDtypeStruct((b, s, D_CONV), MXU_DTYPE),
    )
    row_spec = lambda w: pl.BlockSpec((1, tm, w), lambda bi, j: (bi, j, 0))
    tile_spec = lambda r: pl.BlockSpec((1, 1, r, tm), lambda bi, j: (bi, j, 0, 0))
    return pl.pallas_call(
        _in_proj_kernel,
        grid=(b, ns),
        in_specs=[
            pl.BlockSpec((1, tm, d), lambda bi, j: (bi, j, 0)),
            _resident(wk.shape), _resident(wki.shape), _resident(wqt.shape), _resident(wvt.shape),
            _resident(wqit.shape), _resident(wwit.shape), _resident(wc.shape), _resident(cw.shape),
        ],
        out_specs=(
            row_spec(D_ATTN), row_spec(LANES), tile_spec(N_HEADS * HEAD_PAD), tile_spec(D_ATTN),
            tile_spec(N_IDX_HEADS * HEAD_PAD), tile_spec(N_IDX_HEADS), row_spec(D_CONV),
        ),
        out_shape=out_shape,
        scratch_shapes=[pltpu.VMEM((tm + 8, D_CONV), f32)],
        compiler_params=pltpu.CompilerParams(
            dimension_semantics=("arbitrary", "arbitrary"), vmem_limit_bytes=VMEM_LIMIT_BYTES),
        name="in_proj",
    )(x1, wk, wki, wqt, wvt, wqit, wwit, wc, cw)


def _col_count(mask):
    r, c = mask.shape
    return jnp.sum(jnp.where(mask, 1, 0).astype(jnp.int32).reshape(r // 8, 8, c), axis=0)


def _col_min(x):
    r, c = x.shape
    return jnp.min(x.reshape(r // 8, 8, c), axis=0)


def _dsa_kernel(topk, qt_ref, qit_ref, wit_ref, k_ref, kidx_ref, vt_ref, o_ref,
                s_ref, acc_ref, m_ref, l_ref):
    i = pl.program_id(1)
    kt = SEQ_TILE
    qb = qt_ref.shape[3]
    f32 = jnp.float32
    inf = jnp.float32(jnp.inf)
    n_tiles = i + 1

    seq_len = s_ref.shape[0]
    chunk_shift = CHUNK.bit_length() - 1
    lane_pos = lax.broadcasted_iota(jnp.int32, (1, qb), 1)
    row_pos = lax.broadcasted_iota(jnp.int32, (kt, 1), 0)
    q_pos = i * qb + lane_pos
    n_adm = ((q_pos >> chunk_shift) + 1) * CHUNK

    def tile_rows(t):
        return pl.ds(pl.multiple_of(t * kt, kt), kt)

    def score_tile(t):
        kx = kidx_ref[0, tile_rows(t), :]
        s = jnp.zeros((kt, qb), f32)
        for h in range(N_IDX_HEADS):
            rel = jnp.dot(kx, qit_ref[0, 0, h * HEAD_PAD:(h + 1) * HEAD_PAD, :],
                          preferred_element_type=f32)
            s = s + jnp.maximum(rel, 0.0) * wit_ref[0, 0, h:h + 1, :]
        return s

    def phase_a(t, carry):
        smax, smin = carry
        s = score_tile(t)
        s_ref[tile_rows(t), :] = s
        return jnp.maximum(smax, jnp.max(s, axis=0, keepdims=True)), \
            jnp.minimum(smin, jnp.min(s, axis=0, keepdims=True))

    smax, smin = lax.fori_loop(0, i, phase_a, (jnp.full((1, qb), -inf), jnp.full((1, qb), inf)))
    s = score_tile(i)
    adm = (row_pos >> chunk_shift) <= (lane_pos >> chunk_shift)
    s_ref[tile_rows(i), :] = jnp.where(adm, s, -inf)
    smax = jnp.maximum(smax, jnp.max(jnp.where(adm, s, -inf), axis=0, keepdims=True))
    smin = jnp.minimum(smin, jnp.min(jnp.where(adm, s, inf), axis=0, keepdims=True))

    def sweep(fn, init):
        def body(t, carry):
            return fn(s_ref[tile_rows(t), :], t, carry)
        return lax.fori_loop(0, n_tiles, body, init)

    def count_ge(p):
        c8 = sweep(lambda tile, t, c: c + _col_count(tile >= p), jnp.zeros((8, qb), jnp.int32))
        return jnp.sum(c8, axis=0, keepdims=True)

    def bisect(_, carry):
        lo, hi = carry
        p = lo * 0.5 + hi * 0.5
        up = count_ge(p) >= topk
        return jnp.where(up, p, lo), jnp.where(up, hi, p)

    lo, hi = lax.fori_loop(0, BISECT_ROUNDS, bisect, (smin, smax))
    thr = jnp.min(sweep(lambda tile, t, c: jnp.minimum(c, _col_min(jnp.where(tile >= lo, tile, inf))),
                        jnp.full((8, qb), inf)), axis=0, keepdims=True)

    def above(thr):
        def fn(tile, t, carry):
            gt8, ge8, nx8 = carry
            gt = tile > thr
            return (gt8 + _col_count(gt), ge8 + _col_count(tile >= thr),
                    jnp.minimum(nx8, _col_min(jnp.where(gt, tile, inf))))
        z8 = jnp.zeros((8, qb), jnp.int32)
        gt8, ge8, nx8 = sweep(fn, (z8, z8, jnp.full((8, qb), inf)))
        return (jnp.sum(gt8, axis=0, keepdims=True), jnp.sum(ge8, axis=0, keepdims=True),
                jnp.min(nx8, axis=0, keepdims=True))

    def peel_cond(carry):
        _, n_gt, _, _ = carry
        return jnp.max(n_gt) >= topk

    def peel_body(carry):
        thr, n_gt, _, nxt = carry
        thr = jnp.where(n_gt >= topk, nxt, thr)
        return (thr,) + above(thr)

    thr, n_gt, n_ge, _ = lax.while_loop(peel_cond, peel_body, (thr,) + above(thr))
    thr = jnp.where(n_adm <= topk, smin, thr)
    n_ties_kept = topk - n_gt
    cut_needed = jnp.logical_and(n_adm > topk, (n_ge - n_gt) > n_ties_kept)

    @pl.when(jnp.max(cut_needed.astype(jnp.int32)) > 0)
    def _():
        def ties_upto(c):
            def fn(tile, t, acc):
                pos = t * kt + row_pos
                return acc + _col_count(jnp.logical_and(tile == thr, pos <= c))
            return jnp.sum(sweep(fn, jnp.zeros((8, qb), jnp.int32)), axis=0, keepdims=True)

        def pos_bisect(_, carry):
            lo_p, hi_p = carry
            mid = (lo_p + hi_p) >> 1
            ok = ties_upto(mid) >= n_ties_kept
            return jnp.where(ok, lo_p, mid + 1), jnp.where(ok, mid, hi_p)

        _, cut = lax.fori_loop(0, (seq_len - 1).bit_length(), pos_bisect,
                               (jnp.zeros((1, qb), jnp.int32),
                                jnp.full((1, qb), seq_len - 1, jnp.int32)))

        def knock(t, _):
            tile = s_ref[tile_rows(t), :]
            pos = t * kt + row_pos
            drop = jnp.logical_and(cut_needed, jnp.logical_and(tile == thr, pos > cut))
            s_ref[tile_rows(t), :] = jnp.where(drop, -inf, tile)
            return 0
        lax.fori_loop(0, n_tiles, knock, 0)

    m_ref[...] = jnp.full(m_ref.shape, -inf)
    l_ref[...] = jnp.zeros(l_ref.shape, f32)
    acc_ref[...] = jnp.zeros(acc_ref.shape, f32)
    q_posf = q_pos.astype(f32)

    def phase_c(t, _):
        sel = s_ref[tile_rows(t), :] >= thr
        dist = jnp.abs((t * kt + row_pos).astype(f32) - q_posf)
        for h in range(N_HEADS):
            kh = k_ref[0, tile_rows(t), (h // 2) * LANES:(h // 2 + 1) * LANES]
            logit = jnp.dot(kh, qt_ref[0, 0, h * HEAD_PAD:(h + 1) * HEAD_PAD, :],
                            preferred_element_type=f32)
            slope = 2.0 ** (-8.0 * (h + 1) / N_HEADS)
            logit = jnp.where(sel, logit - slope * dist, -inf)
            m_old = m_ref[h:h + 1, :]
            m_new = jnp.maximum(m_old, jnp.max(logit, axis=0, keepdims=True))
            m_safe = jnp.where(m_new == -inf, 0.0, m_new)
            alpha = jnp.exp(m_old - m_safe)
            p = jnp.exp(logit - m_safe)
            l_ref[h:h + 1, :] = alpha * l_ref[h:h + 1, :] + jnp.sum(p, axis=0, keepdims=True)
            pv = jnp.dot(vt_ref[0, t, h * HEAD_DIM:(h + 1) * HEAD_DIM, :], p.astype(MXU_DTYPE),
                         preferred_element_type=f32)
            rows = slice(h * HEAD_DIM, (h + 1) * HEAD_DIM)
            acc_ref[rows, :] = alpha * acc_ref[rows, :] + pv
            m_ref[h:h + 1, :] = m_new
        return 0

    lax.fori_loop(0, n_tiles, phase_c, 0)

    for h in range(N_HEADS):
        rows = slice(h * HEAD_DIM, (h + 1) * HEAD_DIM)
        acc_ref[rows, :] = acc_ref[rows, :] / l_ref[h:h + 1, :]
    o_ref[0] = acc_ref[...].T.astype(o_ref.dtype)


def _dsa(qt, qit, wit, k, kidx, vt):
    b, ns, _, qb = qt.shape
    s = k.shape[1]
    topk = min(TOPK_MAX, s // 4)
    batch_spec = lambda shape: pl.BlockSpec((1,) + shape[1:], lambda bi, i: (bi,) + (0,) * (len(shape) - 1),
                                            pipeline_mode=pl.Buffered(1))
    tile_spec = lambda r: pl.BlockSpec((1, 1, r, qb), lambda bi, i: (bi, i, 0, 0))
    return pl.pallas_call(
        functools.partial(_dsa_kernel, topk),
        grid=(b, ns),
        in_specs=[
            tile_spec(qt.shape[2]), tile_spec(qit.shape[2]), tile_spec(wit.shape[2]),
            batch_spec(k.shape), batch_spec(kidx.shape), batch_spec(vt.shape),
        ],
        out_specs=pl.BlockSpec((1, qb, D_ATTN), lambda bi, i: (bi, i, 0)),
        out_shape=jax.ShapeDtypeStruct((b, s, D_ATTN), MXU_DTYPE),
        scratch_shapes=[
            pltpu.VMEM((s, qb), jnp.float32),
            pltpu.VMEM((D_ATTN, qb), jnp.float32),
            pltpu.VMEM((N_HEADS, qb), jnp.float32),
            pltpu.VMEM((N_HEADS, qb), jnp.float32),
        ],
        compiler_params=pltpu.CompilerParams(
            dimension_semantics=("arbitrary", "arbitrary"), vmem_limit_bytes=VMEM_LIMIT_BYTES),
        name="dsa",
    )(qt, qit, wit, k, kidx, vt)


def _tail_kernel(alpha, attn_ref, conv_ref, x1_ref, p_ref, woa_ref, woc_ref, g2_ref, b2_ref,
                 wg_ref, wu_ref, wd_ref, g3_ref, b3_ref, wgate_ref, wproj_ref, o_ref):
    f32 = jnp.float32
    mix = jnp.dot(attn_ref[...], woa_ref[...], preferred_element_type=f32) \
        + jnp.dot(conv_ref[...], woc_ref[...], preferred_element_type=f32)
    x2 = _layer_norm(alpha * x1_ref[...] + mix, g2_ref[...], b2_ref[...])
    ffn = _swiglu(x2.astype(MXU_DTYPE), wg_ref, wu_ref, wd_ref)
    x3 = _layer_norm(alpha * x2 + 0.5 * ffn, g3_ref[...], b3_ref[...])
    gate = jax.nn.sigmoid(jnp.dot(x3.astype(MXU_DTYPE), wgate_ref[...], preferred_element_type=f32))
    proj = jnp.dot(p_ref[...].astype(MXU_DTYPE), wproj_ref[...], preferred_element_type=f32)
    o_ref[...] = x3 + gate * proj


def _tail(attn, conv, x1, p, woa, woc, g2, b2, wg3, wu3, wd3, g3, b3, wgate, wproj, alpha):
    n, d = x1.shape
    tm = min(ROW_TILE, n)
    row = lambda w: pl.BlockSpec((tm, w), lambda i: (i, 0))
    weights = (woa, woc, g2, b2, wg3, wu3, wd3, g3, b3, wgate, wproj)
    return pl.pallas_call(
        functools.partial(_tail_kernel, alpha),
        grid=(n // tm,),
        in_specs=[row(attn.shape[1]), row(conv.shape[1]), row(d), row(p.shape[1])]
        + [_resident(w.shape) for w in weights],
        out_specs=row(d),
        out_shape=jax.ShapeDtypeStruct((n, d), jnp.float32),
        compiler_params=pltpu.CompilerParams(
            dimension_semantics=("arbitrary",), vmem_limit_bytes=VMEM_LIMIT_BYTES),
        name="tail",
    )(attn, conv, x1, p, *weights)


def _ffn_weights(wg, wu, wd):
    d, f = wg.shape
    n = f // FF_CHUNK
    chunk_cols = lambda w: w.reshape(d, n, FF_CHUNK).transpose(1, 0, 2).astype(MXU_DTYPE)
    return chunk_cols(wg), chunk_cols(wu), wd.reshape(n, FF_CHUNK, d).astype(MXU_DTYPE)


def _padded_head_rows(w, parity_of_head):
    d = w.shape[0]
    wt = w.T.reshape(-1, HEAD_DIM, d)
    zero = jnp.zeros((HEAD_DIM, d), w.dtype)
    groups = []
    for h in range(wt.shape[0]):
        groups += [wt[h], zero] if parity_of_head(h) == 0 else [zero, wt[h]]
    return jnp.concatenate(groups, axis=0).astype(MXU_DTYPE)


def kernel(x, p, ln1_g, ln1_b, ffn1_wg, ffn1_wu, ffn1_wd, w_in, conv_w, w_out, ln2_g, ln2_b,
           ffn2_wg, ffn2_wu, ffn2_wd, ln3_g, ln3_b, ple_gate_w, ple_proj_w):
    b, s, d = x.shape
    depth = p.shape[0]
    alpha = (2 * depth) ** 0.25
    assert d == D_MODEL and s % SEQ_TILE == 0 and (b * s) % min(ROW_TILE, b * s) == 0
    o_q, o_k, o_v, o_qi = 0, D_ATTN, 2 * D_ATTN, 3 * D_ATTN
    o_ki = o_qi + N_IDX_HEADS * IDX_DIM
    o_wi = o_ki + IDX_DIM
    o_conv = o_wi + N_IDX_HEADS
    row2 = lambda v: v.reshape(1, -1)

    for i in range(depth):
        wi = w_in[i]
        x1 = _ffn_ln(x.reshape(b * s, d), *_ffn_weights(ffn1_wg[i], ffn1_wu[i], ffn1_wd[i]),
                     row2(ln1_g[i]), row2(ln1_b[i]), alpha)
        wk = wi[:, o_k:o_v].astype(MXU_DTYPE)
        wki = jnp.pad(wi[:, o_ki:o_wi], ((0, 0), (0, LANES - IDX_DIM))).astype(MXU_DTYPE)
        wqt = _padded_head_rows(wi[:, o_q:o_k] * (HEAD_DIM ** -0.5), lambda h: h % 2)
        wvt = wi[:, o_v:o_qi].T.astype(MXU_DTYPE)
        wqit = _padded_head_rows(wi[:, o_qi:o_ki], lambda h: 0)
        wwit = jnp.pad(wi[:, o_wi:o_conv].T, ((0, LANES - N_IDX_HEADS), (0, 0))).astype(MXU_DTYPE)
        wc = wi[:, o_conv:].astype(MXU_DTYPE)
        k, kidx, qt, vt, qit, wit, conv = _in_proj(
            x1.reshape(b, s, d), wk, wki, wqt, wvt, wqit, wwit, wc, conv_w[i])
        attn = _dsa(qt, qit, wit, k, kidx, vt)
        wo = w_out[i].astype(MXU_DTYPE)
        x = _tail(attn.reshape(b * s, D_ATTN), conv.reshape(b * s, D_CONV), x1, p[i].reshape(b * s, -1),
                  wo[:D_ATTN], wo[D_ATTN:], row2(ln2_g[i]), row2(ln2_b[i]),
                  *_ffn_weights(ffn2_wg[i], ffn2_wu[i], ffn2_wd[i]),
                  row2(ln3_g[i]), row2(ln3_b[i]),
                  ple_gate_w[i].astype(MXU_DTYPE), ple_proj_w[i].astype(MXU_DTYPE), alpha).reshape(b, s, d)
    return x
```

```python
import functools

import jax
import jax.numpy as jnp
from jax import lax
from jax.experimental import pallas as pl
from jax.experimental.pallas import tpu as pltpu

D_MODEL = 1024
CHUNK = 64
HEAD_DIM = 64
N_HEADS = 8
D_ATTN = N_HEADS * HEAD_DIM
N_IDX_HEADS = 8
IDX_DIM = 64
TOPK_MAX = 256
D_CONV = D_MODEL - D_ATTN
CONV_WIDTH = 3
D_FF = 2816
D_PLE = 256
LN_EPS = 1e-5
IDX_SCALE = (N_IDX_HEADS ** -0.5) * (IDX_DIM ** -0.5)

MXU_DTYPE = jnp.bfloat16

LANES = 128
FF_CHUNK = 256
ROW_TILE = 512
SEQ_TILE = 256
HEAD_PAD = 2 * HEAD_DIM
V_PAD = HEAD_DIM + 16
POS_ROWS = 16
MAX_BISECT_ROUNDS = 40
VMEM_LIMIT_BYTES = 56 * 1024 * 1024


def _layer_norm(y, g, b):
    mu = jnp.mean(y, axis=-1, keepdims=True)
    yc = y - mu
    var = jnp.mean(yc * yc, axis=-1, keepdims=True)
    return yc * lax.rsqrt(var + LN_EPS) * g + b


def _swiglu(xb, wg_ref, wu_ref, wd_ref):
    n_chunks = wg_ref.shape[0]

    def body(c, acc):
        g = jnp.dot(xb, wg_ref[c], preferred_element_type=jnp.float32)
        u = jnp.dot(xb, wu_ref[c], preferred_element_type=jnp.float32)
        a = (g * jax.nn.sigmoid(g)) * u
        return acc + jnp.dot(a.astype(MXU_DTYPE), wd_ref[c], preferred_element_type=jnp.float32)

    return lax.fori_loop(0, n_chunks, body, jnp.zeros((xb.shape[0], wd_ref.shape[2]), jnp.float32))


def _ffn_ln_kernel(alpha, x_ref, wg_ref, wu_ref, wd_ref, g_ref, b_ref, o_ref):
    x = x_ref[...]
    ffn = _swiglu(x.astype(MXU_DTYPE), wg_ref, wu_ref, wd_ref)
    o_ref[...] = _layer_norm(alpha * x + 0.5 * ffn, g_ref[...], b_ref[...])


def _resident(shape):
    nd = len(shape)
    return pl.BlockSpec(shape, lambda *_: (0,) * nd, pipeline_mode=pl.Buffered(1))


def _ffn_ln(x2d, wg3, wu3, wd3, g, b, alpha):
    n, d = x2d.shape
    tm = min(ROW_TILE, n)
    return pl.pallas_call(
        functools.partial(_ffn_ln_kernel, alpha),
        grid=(n // tm,),
        in_specs=[
            pl.BlockSpec((tm, d), lambda i: (i, 0)),
            _resident(wg3.shape), _resident(wu3.shape), _resident(wd3.shape),
            _resident(g.shape), _resident(b.shape),
        ],
        out_specs=pl.BlockSpec((tm, d), lambda i: (i, 0)),
        out_shape=jax.ShapeDtypeStruct((n, d), jnp.float32),
        compiler_params=pltpu.CompilerParams(
            dimension_semantics=("arbitrary",), vmem_limit_bytes=VMEM_LIMIT_BYTES),
        name="ffn_ln",
    )(x2d, wg3, wu3, wd3, g, b)


def _in_proj_kernel(x_ref, wk_ref, wki_ref, wqt_ref, wvt_ref, wqit_ref, wwit_ref, wc_ref, cw_ref,
                    k_ref, kidx_ref, qt_ref, vt_ref, qit_ref, wit_ref, conv_ref, zbuf_ref):
    j = pl.program_id(1)
    tm = x_ref.shape[1]
    xb = x_ref[0].astype(MXU_DTYPE)
    nt = (((1,), (1,)), ((), ()))

    k_ref[0] = jnp.dot(xb, wk_ref[...], preferred_element_type=jnp.float32).astype(k_ref.dtype)
    kidx_ref[0] = jnp.dot(xb, wki_ref[...], preferred_element_type=jnp.float32).astype(kidx_ref.dtype)
    qt_ref[0, 0] = lax.dot_general(wqt_ref[...], xb, nt,
                                   preferred_element_type=jnp.float32).astype(qt_ref.dtype)
    vt = lax.dot_general(wvt_ref[...], xb, nt, preferred_element_type=jnp.float32)
    ones = jnp.ones((V_PAD - HEAD_DIM, tm), jnp.float32)
    vt_ref[0, 0] = jnp.concatenate(
        [piece for h in range(N_HEADS) for piece in (vt[h * HEAD_DIM:(h + 1) * HEAD_DIM], ones)],
        axis=0).astype(vt_ref.dtype)
    qit_ref[0, 0] = lax.dot_general(wqit_ref[...], xb, nt,
                                    preferred_element_type=jnp.float32).astype(qit_ref.dtype)
    wit = lax.dot_general(wwit_ref[...], xb, nt, preferred_element_type=jnp.float32)
    wit_ref[0, 0] = wit[0:N_IDX_HEADS, :] * IDX_SCALE

    h = jnp.dot(xb, wc_ref[...], preferred_element_type=jnp.float32)
    bg = h[:, 0:D_CONV]
    z = h[:, D_CONV:2 * D_CONV] * h[:, 2 * D_CONV:3 * D_CONV]

    @pl.when(j == 0)
    def _():
        zbuf_ref[0:8, :] = jnp.zeros((8, D_CONV), jnp.float32)

    zbuf_ref[8:8 + tm, :] = z
    z1 = zbuf_ref[7:7 + tm, :]
    z2 = zbuf_ref[6:6 + tm, :]
    y = (cw_ref[0:1, :] * z2 + cw_ref[1:2, :] * z1) + cw_ref[2:3, :] * z
    conv_ref[0] = (bg * y).astype(conv_ref.dtype)
    zbuf_ref[0:8, :] = z[tm - 8:tm, :]


def _in_proj(x1, wk, wki, wqt, wvt, wqit, wwit, wc, cw):
    b, s, d = x1.shape
    tm = SEQ_TILE
    ns = s // tm
    f32 = jnp.float32
    out_shape = (
        jax.ShapeDtypeStruct((b, s, D_ATTN), MXU_DTYPE),
        jax.ShapeDtypeStruct((b, s, LANES), MXU_DTYPE),
        jax.ShapeDtypeStruct((b, ns, N_HEADS * HEAD_PAD, tm), MXU_DTYPE),
        jax.ShapeDtypeStruct((b, ns, N_HEADS * V_PAD, tm), MXU_DTYPE),
        jax.ShapeDtypeStruct((b, ns, N_IDX_HEADS * HEAD_PAD, tm), MXU_DTYPE),
        jax.ShapeDtypeStruct((b, ns, N_IDX_HEADS, tm), f32),
        jax.ShapeDtypeStruct((b, s, D_CONV), MXU_DTYPE),
    )
    row_spec = lambda w: pl.BlockSpec((1, tm, w), lambda bi, j: (bi, j, 0))
    tile_spec = lambda r: pl.BlockSpec((1, 1, r, tm), lambda bi, j: (bi, j, 0, 0))
    return pl.pallas_call(
        _in_proj_kernel,
        grid=(b, ns),
        in_specs=[
            pl.BlockSpec((1, tm, d), lambda bi, j: (bi, j, 0)),
            _resident(wk.shape), _resident(wki.shape), _resident(wqt.shape), _resident(wvt.shape),
            _resident(wqit.shape), _resident(wwit.shape), _resident(wc.shape), _resident(cw.shape),
        ],
        out_specs=(
            row_spec(D_ATTN), row_spec(LANES), tile_spec(N_HEADS * HEAD_PAD), tile_spec(N_HEADS * V_PAD),
            tile_spec(N_IDX_HEADS * HEAD_PAD), tile_spec(N_IDX_HEADS), row_spec(D_CONV),
        ),
        out_shape=out_shape,
        scratch_shapes=[pltpu.VMEM((tm + 8, D_CONV), f32)],
        compiler_params=pltpu.CompilerParams(
            dimension_semantics=("arbitrary", "arbitrary"), vmem_limit_bytes=VMEM_LIMIT_BYTES),
        name="in_proj",
    )(x1, wk, wki, wqt, wvt, wqit, wwit, wc, cw)


def _tree_reduce(op, x):
    r, c = x.shape
    parts = [x[g * 8:(g + 1) * 8, :] for g in range(r // 8)]
    while len(parts) > 1:
        parts = [op(parts[j], parts[j + 1]) for j in range(0, len(parts) - 1, 2)] + parts[len(parts) & ~1:]
    return parts[0]


def _col_count(mask):
    return _tree_reduce(jnp.add, jnp.where(mask, 1, 0).astype(jnp.int32))


def _col_min(x):
    return _tree_reduce(jnp.minimum, x)


def _dsa_kernel(topk, qt_ref, qit_ref, wit_ref, k_ref, kidx_ref, vt_ref, pf_ref, o_ref,
                s_ref, acc_ref, rhs_ref, lga_ref, lgb_ref):
    i = pl.program_id(1)
    kt = SEQ_TILE
    qb = qt_ref.shape[3]
    f32 = jnp.float32
    inf = jnp.float32(jnp.inf)
    n_tiles = i + 1

    seq_len = s_ref.shape[0]
    chunk_shift = CHUNK.bit_length() - 1
    lane_pos = lax.broadcasted_iota(jnp.int32, (1, qb), 1)
    row_pos = lax.broadcasted_iota(jnp.int32, (kt, 1), 0)
    q_pos = i * qb + lane_pos
    n_adm = ((q_pos >> chunk_shift) + 1) * CHUNK

    def tile_rows(t):
        return pl.ds(pl.multiple_of(t * kt, kt), kt)

    def score_tile(t):
        kx = kidx_ref[0, tile_rows(t), :]
        s = jnp.zeros((kt, qb), f32)
        for h in range(N_IDX_HEADS):
            rel = jnp.dot(kx, qit_ref[0, 0, h * HEAD_PAD:(h + 1) * HEAD_PAD, :],
                          preferred_element_type=f32)
            s = s + jnp.maximum(rel, 0.0) * wit_ref[0, 0, h:h + 1, :]
        return s

    def phase_a(t, carry):
        smax, smin = carry
        s = score_tile(t)
        s_ref[tile_rows(t), :] = s
        return jnp.maximum(smax, jnp.max(s, axis=0, keepdims=True)), \
            jnp.minimum(smin, jnp.min(s, axis=0, keepdims=True))

    smax, smin = lax.fori_loop(0, i, phase_a, (jnp.full((1, qb), -inf), jnp.full((1, qb), inf)))
    s = score_tile(i)
    adm = (row_pos >> chunk_shift) <= (lane_pos >> chunk_shift)
    s_ref[tile_rows(i), :] = jnp.where(adm, s, -inf)
    smax = jnp.maximum(smax, jnp.max(jnp.where(adm, s, -inf), axis=0, keepdims=True))
    smin = jnp.minimum(smin, jnp.min(jnp.where(adm, s, inf), axis=0, keepdims=True))

    def sweep(fn, init):
        def body(t, carry):
            return fn(s_ref[tile_rows(t), :], t, carry)
        return lax.fori_loop(0, n_tiles, body, init)

    def count_ge(p):
        c8 = sweep(lambda tile, t, c: c + _col_count(tile >= p), jnp.zeros((8, qb), jnp.int32))
        return jnp.sum(c8, axis=0, keepdims=True)

    searching = n_adm > topk

    def midpoint(lo, hi):
        return lo * 0.5 + hi * 0.5

    def live(lo, hi, c_lo):
        p = midpoint(lo, hi)
        return jnp.logical_and(jnp.logical_and(searching, c_lo != topk), jnp.logical_and(p > lo, p < hi))

    def bisect_cond(carry):
        lo, hi, c_lo, rounds = carry
        return jnp.logical_and(rounds < MAX_BISECT_ROUNDS, jnp.max(live(lo, hi, c_lo).astype(jnp.int32)) > 0)

    def bisect_body(carry):
        lo, hi, c_lo, rounds = carry
        p = midpoint(lo, hi)
        c = count_ge(p)
        act = live(lo, hi, c_lo)
        up = jnp.logical_and(act, c >= topk)
        down = jnp.logical_and(act, c < topk)
        return jnp.where(up, p, lo), jnp.where(down, p, hi), jnp.where(up, c, c_lo), rounds + 1

    lo, _, c_lo, _ = lax.while_loop(bisect_cond, bisect_body, (smin, smax, n_adm, jnp.int32(0)))
    thr = jnp.min(sweep(lambda tile, t, c: jnp.minimum(c, _col_min(jnp.where(tile >= lo, tile, inf))),
                        jnp.full((8, qb), inf)), axis=0, keepdims=True)

    def above(thr):
        def fn(tile, t, carry):
            gt8, ge8, nx8 = carry
            gt = tile > thr
            return (gt8 + _col_count(gt), ge8 + _col_count(tile >= thr),
                    jnp.minimum(nx8, _col_min(jnp.where(gt, tile, inf))))
        z8 = jnp.zeros((8, qb), jnp.int32)
        gt8, ge8, nx8 = sweep(fn, (z8, z8, jnp.full((8, qb), inf)))
        return (jnp.sum(gt8, axis=0, keepdims=True), jnp.sum(ge8, axis=0, keepdims=True),
                jnp.min(nx8, axis=0, keepdims=True))

    def settle(thr):
        def peel_cond(carry):
            _, n_gt, _, _ = carry
            return jnp.max(n_gt) >= topk

        def peel_body(carry):
            thr, n_gt, _, nxt = carry
            thr = jnp.where(n_gt >= topk, nxt, thr)
            return (thr,) + above(thr)

        thr, n_gt, n_ge, _ = lax.while_loop(peel_cond, peel_body, (thr,) + above(thr))
        n_ties_kept = topk - n_gt
        cut_needed = jnp.logical_and(searching, n_ge > topk)

        @pl.when(jnp.max(cut_needed.astype(jnp.int32)) > 0)
        def _():
            def ties_upto(c):
                def fn(tile, t, acc):
                    pos = t * kt + row_pos
                    return acc + _col_count(jnp.logical_and(tile == thr, pos <= c))
                return jnp.sum(sweep(fn, jnp.zeros((8, qb), jnp.int32)), axis=0, keepdims=True)

            def pos_bisect(_, carry):
                lo_p, hi_p = carry
                mid = (lo_p + hi_p) >> 1
                ok = ties_upto(mid) >= n_ties_kept
                return jnp.where(ok, lo_p, mid + 1), jnp.where(ok, mid, hi_p)

            _, cut = lax.fori_loop(0, (seq_len - 1).bit_length(), pos_bisect,
                                   (jnp.zeros((1, qb), jnp.int32),
                                    jnp.full((1, qb), seq_len - 1, jnp.int32)))

            def knock(t, _):
                tile = s_ref[tile_rows(t), :]
                pos = t * kt + row_pos
                drop = jnp.logical_and(cut_needed, jnp.logical_and(tile == thr, pos > cut))
                s_ref[tile_rows(t), :] = jnp.where(drop, -inf, tile)
                return 0
            lax.fori_loop(0, n_tiles, knock, 0)

        return thr

    unsettled = jnp.logical_and(searching, c_lo != topk)
    thr = lax.cond(jnp.max(unsettled.astype(jnp.int32)) > 0, settle, lambda t: t, thr)
    thr = jnp.where(searching, thr, smin)


    acc_ref[...] = jnp.zeros(acc_ref.shape, f32)
    slopes = [2.0 ** (-8.0 * (h + 1) / N_HEADS) for h in range(N_HEADS)]
    pos_row = lax.broadcasted_iota(jnp.int32, (POS_ROWS, 1), 0)
    q_hi = (q_pos >> chunk_shift).astype(f32)
    q_lo = (q_pos & (CHUNK - 1)).astype(f32)
    for h in range(N_HEADS):
        pos_rows = jnp.where(pos_row == 0, slopes[h] * CHUNK,
                             jnp.where(pos_row == 1, slopes[h],
                                       jnp.where(pos_row == 2, -slopes[h] * CHUNK * q_hi,
                                                 jnp.where(pos_row == 3, -slopes[h] * q_lo, 0.0))))
        rhs_ref[h, 0:HEAD_PAD, :] = qt_ref[0, 0, h * HEAD_PAD:(h + 1) * HEAD_PAD, :]
        rhs_ref[h, HEAD_PAD:HEAD_PAD + POS_ROWS, :] = pos_rows.astype(rhs_ref.dtype)
        rhs_ref[h, HEAD_PAD + POS_ROWS:, :] = jnp.zeros((LANES - POS_ROWS, qb), rhs_ref.dtype)

    def logits_stage(t, lg, thr_t, diagonal=False):
        rows = tile_rows(t)
        mask_bias = jnp.where(s_ref[rows, :] >= thr_t, 0.0, -inf)
        pf = pf_ref[rows, :]
        if diagonal:
            twice_future = 2.0 * jnp.maximum(row_pos - lane_pos, 0).astype(f32)
        tile_max = []
        for h in range(N_HEADS):
            lhs = jnp.concatenate([k_ref[0, rows, (h // 2) * LANES:(h // 2 + 1) * LANES], pf], axis=1)
            logit = jnp.dot(lhs, rhs_ref[h], preferred_element_type=f32) + mask_bias
            if diagonal:
                logit = logit - slopes[h] * twice_future
            lg[h] = logit
            tile_max.append(jnp.max(logit, axis=0, keepdims=True))
        return jnp.concatenate(tile_max, axis=0)

    def softmax_stage(t, lg, m_all, tile_max):
        m_rows = []
        for h in range(N_HEADS):
            m_old = m_all[h:h + 1, :]
            m_new = jnp.maximum(m_old, tile_max[h:h + 1, :])
            m_safe = jnp.where(m_new == -inf, 0.0, m_new)
            alpha = jnp.exp(m_old - m_safe)
            p = jnp.exp(lg[h] - m_safe).astype(MXU_DTYPE)
            hrows = slice(h * V_PAD, (h + 1) * V_PAD)
            pv = jnp.dot(vt_ref[0, t, hrows, :], p, preferred_element_type=f32)
            acc_ref[hrows, :] = alpha * acc_ref[hrows, :] + pv
            m_rows.append(m_new)
        return jnp.concatenate(m_rows, axis=0)

    def tile_pair(j, carry):
        m_all, max_a = carry
        t0 = 2 * j
        t1 = t0 + 1
        max_b = logits_stage(t1, lgb_ref, jnp.where(t1 < i, thr, inf))
        m_all = softmax_stage(t0, lga_ref, m_all, max_a)
        max_a = logits_stage(jnp.minimum(t0 + 2, i), lga_ref, thr)
        m_all = softmax_stage(t1, lgb_ref, m_all, max_b)
        return m_all, max_a

    m_all, _ = lax.fori_loop(0, (i + 1) >> 1, tile_pair,
                             (jnp.full((N_HEADS, qb), -inf), logits_stage(0, lga_ref, thr)))
    softmax_stage(i, lga_ref, m_all, logits_stage(i, lga_ref, thr, diagonal=True))


    out = [acc_ref[h * V_PAD:h * V_PAD + HEAD_DIM, :] / acc_ref[h * V_PAD + HEAD_DIM:h * V_PAD + HEAD_DIM + 1, :]
           for h in range(N_HEADS)]
    o_ref[0] = jnp.concatenate(out, axis=0).T.astype(o_ref.dtype)


def _dsa(qt, qit, wit, k, kidx, vt, pf):
    b, ns, _, qb = qt.shape
    s = k.shape[1]
    topk = min(TOPK_MAX, s // 4)
    batch_spec = lambda shape: pl.BlockSpec((1,) + shape[1:], lambda bi, i: (bi,) + (0,) * (len(shape) - 1),
                                            pipeline_mode=pl.Buffered(1))
    tile_spec = lambda r: pl.BlockSpec((1, 1, r, qb), lambda bi, i: (bi, i, 0, 0))
    return pl.pallas_call(
        functools.partial(_dsa_kernel, topk),
        grid=(b, ns),
        in_specs=[
            tile_spec(qt.shape[2]), tile_spec(qit.shape[2]), tile_spec(wit.shape[2]),
            batch_spec(k.shape), batch_spec(kidx.shape), batch_spec(vt.shape), _resident(pf.shape),
        ],
        out_specs=pl.BlockSpec((1, qb, D_ATTN), lambda bi, i: (bi, i, 0)),
        out_shape=jax.ShapeDtypeStruct((b, s, D_ATTN), MXU_DTYPE),
        scratch_shapes=[
            pltpu.VMEM((s, qb), jnp.float32),
            pltpu.VMEM((N_HEADS * V_PAD, qb), jnp.float32),
            pltpu.VMEM((N_HEADS, 2 * LANES, qb), MXU_DTYPE),
            pltpu.VMEM((N_HEADS, SEQ_TILE, qb), jnp.float32),
            pltpu.VMEM((N_HEADS, SEQ_TILE, qb), jnp.float32),
        ],
        compiler_params=pltpu.CompilerParams(
            dimension_semantics=("arbitrary", "arbitrary"), vmem_limit_bytes=VMEM_LIMIT_BYTES),
        name="dsa",
    )(qt, qit, wit, k, kidx, vt, pf)


def _tail_kernel(alpha, attn_ref, conv_ref, x1_ref, p_ref, woa_ref, woc_ref, g2_ref, b2_ref,
                 wg_ref, wu_ref, wd_ref, g3_ref, b3_ref, wgate_ref, wproj_ref, o_ref):
    f32 = jnp.float32
    mix = jnp.dot(attn_ref[...], woa_ref[...], preferred_element_type=f32) \
        + jnp.dot(conv_ref[...], woc_ref[...], preferred_element_type=f32)
    x2 = _layer_norm(alpha * x1_ref[...] + mix, g2_ref[...], b2_ref[...])
    ffn = _swiglu(x2.astype(MXU_DTYPE), wg_ref, wu_ref, wd_ref)
    x3 = _layer_norm(alpha * x2 + 0.5 * ffn, g3_ref[...], b3_ref[...])
    gate = jax.nn.sigmoid(jnp.dot(x3.astype(MXU_DTYPE), wgate_ref[...], preferred_element_type=f32))
    proj = jnp.dot(p_ref[...].astype(MXU_DTYPE), wproj_ref[...], preferred_element_type=f32)
    o_ref[...] = x3 + gate * proj


def _tail(attn, conv, x1, p, woa, woc, g2, b2, wg3, wu3, wd3, g3, b3, wgate, wproj, alpha):
    n, d = x1.shape
    tm = min(ROW_TILE, n)
    row = lambda w: pl.BlockSpec((tm, w), lambda i: (i, 0))
    weights = (woa, woc, g2, b2, wg3, wu3, wd3, g3, b3, wgate, wproj)
    return pl.pallas_call(
        functools.partial(_tail_kernel, alpha),
        grid=(n // tm,),
        in_specs=[row(attn.shape[1]), row(conv.shape[1]), row(d), row(p.shape[1])]
        + [_resident(w.shape) for w in weights],
        out_specs=row(d),
        out_shape=jax.ShapeDtypeStruct((n, d), jnp.float32),
        compiler_params=pltpu.CompilerParams(
            dimension_semantics=("arbitrary",), vmem_limit_bytes=VMEM_LIMIT_BYTES),
        name="tail",
    )(attn, conv, x1, p, *weights)


def _ffn_weights(wg, wu, wd):
    d, f = wg.shape
    n = f // FF_CHUNK
    chunk_cols = lambda w: w.reshape(d, n, FF_CHUNK).transpose(1, 0, 2).astype(MXU_DTYPE)
    return chunk_cols(wg), chunk_cols(wu), wd.reshape(n, FF_CHUNK, d).astype(MXU_DTYPE)


def _padded_head_rows(w, parity_of_head):
    d = w.shape[0]
    wt = w.T.reshape(-1, HEAD_DIM, d)
    zero = jnp.zeros((HEAD_DIM, d), w.dtype)
    groups = []
    for h in range(wt.shape[0]):
        groups += [wt[h], zero] if parity_of_head(h) == 0 else [zero, wt[h]]
    return jnp.concatenate(groups, axis=0).astype(MXU_DTYPE)


def _key_position_features(s):
    pos = jnp.arange(s, dtype=jnp.int32)[:, None]
    col = jnp.arange(LANES, dtype=jnp.int32)[None, :]
    feat = jnp.where(col == 0, pos // CHUNK, jnp.where(col == 1, pos % CHUNK, jnp.where(col < 4, 1, 0)))
    return feat.astype(MXU_DTYPE)


def kernel(x, p, ln1_g, ln1_b, ffn1_wg, ffn1_wu, ffn1_wd, w_in, conv_w, w_out, ln2_g, ln2_b,
           ffn2_wg, ffn2_wu, ffn2_wd, ln3_g, ln3_b, ple_gate_w, ple_proj_w):
    b, s, d = x.shape
    depth = p.shape[0]
    alpha = (2 * depth) ** 0.25
    assert d == D_MODEL and s % SEQ_TILE == 0 and (b * s) % min(ROW_TILE, b * s) == 0
    o_q, o_k, o_v, o_qi = 0, D_ATTN, 2 * D_ATTN, 3 * D_ATTN
    o_ki = o_qi + N_IDX_HEADS * IDX_DIM
    o_wi = o_ki + IDX_DIM
    o_conv = o_wi + N_IDX_HEADS
    row2 = lambda v: v.reshape(1, -1)

    for i in range(depth):
        wi = w_in[i]
        x1 = _ffn_ln(x.reshape(b * s, d), *_ffn_weights(ffn1_wg[i], ffn1_wu[i], ffn1_wd[i]),
                     row2(ln1_g[i]), row2(ln1_b[i]), alpha)
        wk = wi[:, o_k:o_v].astype(MXU_DTYPE)
        wki = jnp.pad(wi[:, o_ki:o_wi], ((0, 0), (0, LANES - IDX_DIM))).astype(MXU_DTYPE)
        wqt = _padded_head_rows(wi[:, o_q:o_k] * (HEAD_DIM ** -0.5), lambda h: h % 2)
        wvt = wi[:, o_v:o_qi].T.astype(MXU_DTYPE)
        wqit = _padded_head_rows(wi[:, o_qi:o_ki], lambda h: 0)
        wwit = jnp.pad(wi[:, o_wi:o_conv].T, ((0, LANES - N_IDX_HEADS), (0, 0))).astype(MXU_DTYPE)
        wc = wi[:, o_conv:].astype(MXU_DTYPE)
        k, kidx, qt, vt, qit, wit, conv = _in_proj(
            x1.reshape(b, s, d), wk, wki, wqt, wvt, wqit, wwit, wc, conv_w[i])
        attn = _dsa(qt, qit, wit, k, kidx, vt, _key_position_features(s))
        wo = w_out[i].astype(MXU_DTYPE)
        x = _tail(attn.reshape(b * s, D_ATTN), conv.reshape(b * s, D_CONV), x1, p[i].reshape(b * s, -1),
                  wo[:D_ATTN], wo[D_ATTN:], row2(ln2_g[i]), row2(ln2_b[i]),
                  *_ffn_weights(ffn2_wg[i], ffn2_wu[i], ffn2_wd[i]),
                  row2(ln3_g[i]), row2(ln3_b[i]),
                  ple_gate_w[i].astype(MXU_DTYPE), ple_proj_w[i].astype(MXU_DTYPE), alpha).reshape(b, s, d)
    return x
```

```python
import functools

import jax
import jax.numpy as jnp
from jax import lax
from jax.experimental import pallas as pl
from jax.experimental.pallas import tpu as pltpu

D_MODEL = 1024
CHUNK = 64
HEAD_DIM = 64
N_HEADS = 8
D_ATTN = N_HEADS * HEAD_DIM
N_IDX_HEADS = 8
IDX_DIM = 64
TOPK_MAX = 256
D_CONV = D_MODEL - D_ATTN
CONV_WIDTH = 3
D_FF = 2816
D_PLE = 256
LN_EPS = 1e-5
IDX_SCALE = (N_IDX_HEADS ** -0.5) * (IDX_DIM ** -0.5)

MXU_DTYPE = jnp.bfloat16

LANES = 128
FF_CHUNK = 256
ROW_TILE = 512
SEQ_TILE = 256
HEAD_PAD = 2 * HEAD_DIM
V_PAD = HEAD_DIM + 16
POS_ROWS = 16
REDUCE_CHAINS = 4
BISECT_ROUNDS = 18
VMEM_LIMIT_BYTES = 56 * 1024 * 1024


def _layer_norm(y, g, b):
    mu = jnp.mean(y, axis=-1, keepdims=True)
    yc = y - mu
    var = jnp.mean(yc * yc, axis=-1, keepdims=True)
    return yc * lax.rsqrt(var + LN_EPS) * g + b


def _swiglu(xb, wg_ref, wu_ref, wd_ref, act_ref):
    for c in range(wg_ref.shape[0]):
        g = jnp.dot(xb, wg_ref[c], preferred_element_type=jnp.float32)
        u = jnp.dot(xb, wu_ref[c], preferred_element_type=jnp.float32)
        act_ref[:, c * FF_CHUNK:(c + 1) * FF_CHUNK] = ((g * jax.nn.sigmoid(g)) * u).astype(act_ref.dtype)
    return jnp.dot(act_ref[...], wd_ref[...], preferred_element_type=jnp.float32)


def _ffn_ln_kernel(alpha, x_ref, wg_ref, wu_ref, wd_ref, g_ref, b_ref, o_ref, act_ref):
    x = x_ref[...]
    ffn = _swiglu(x.astype(MXU_DTYPE), wg_ref, wu_ref, wd_ref, act_ref)
    o_ref[...] = _layer_norm(alpha * x + 0.5 * ffn, g_ref[...], b_ref[...])


def _resident(shape):
    nd = len(shape)
    return pl.BlockSpec(shape, lambda *_: (0,) * nd, pipeline_mode=pl.Buffered(1))


def _ffn_ln(x2d, wg3, wu3, wd3, g, b, alpha):
    n, d = x2d.shape
    tm = min(ROW_TILE, n)
    return pl.pallas_call(
        functools.partial(_ffn_ln_kernel, alpha),
        grid=(n // tm,),
        in_specs=[
            pl.BlockSpec((tm, d), lambda i: (i, 0)),
            _resident(wg3.shape), _resident(wu3.shape), _resident(wd3.shape),
            _resident(g.shape), _resident(b.shape),
        ],
        out_specs=pl.BlockSpec((tm, d), lambda i: (i, 0)),
        out_shape=jax.ShapeDtypeStruct((n, d), jnp.float32),
        scratch_shapes=[pltpu.VMEM((tm, wd3.shape[0]), MXU_DTYPE)],
        compiler_params=pltpu.CompilerParams(
            dimension_semantics=("arbitrary",), vmem_limit_bytes=VMEM_LIMIT_BYTES),
        name="ffn_ln",
    )(x2d, wg3, wu3, wd3, g, b)


def _in_proj_kernel(x_ref, wk_ref, wki_ref, wqt_ref, wvt_ref, wqit_ref, wwit_ref, wc_ref, cw_ref,
                    k_ref, kidx_ref, qt_ref, vt_ref, qit_ref, wit_ref, conv_ref, zbuf_ref):
    j = pl.program_id(1)
    tm = x_ref.shape[1]
    xb = x_ref[0].astype(MXU_DTYPE)
    nt = (((1,), (1,)), ((), ()))

    k_ref[0] = jnp.dot(xb, wk_ref[...], preferred_element_type=jnp.float32).astype(k_ref.dtype)
    kidx_ref[0] = jnp.dot(xb, wki_ref[...], preferred_element_type=jnp.float32).astype(kidx_ref.dtype)
    qt_ref[0, 0] = lax.dot_general(wqt_ref[...], xb, nt,
                                   preferred_element_type=jnp.float32).astype(qt_ref.dtype)
    vt = lax.dot_general(wvt_ref[...], xb, nt, preferred_element_type=jnp.float32)
    ones = jnp.ones((V_PAD - HEAD_DIM, tm), jnp.float32)
    vt_ref[0, 0] = jnp.concatenate(
        [piece for h in range(N_HEADS) for piece in (vt[h * HEAD_DIM:(h + 1) * HEAD_DIM], ones)],
        axis=0).astype(vt_ref.dtype)
    qit_ref[0, 0] = lax.dot_general(wqit_ref[...], xb, nt,
                                    preferred_element_type=jnp.float32).astype(qit_ref.dtype)
    wit = lax.dot_general(wwit_ref[...], xb, nt, preferred_element_type=jnp.float32)
    wit_ref[0, 0] = wit[0:N_IDX_HEADS, :] * IDX_SCALE

    h = jnp.dot(xb, wc_ref[...], preferred_element_type=jnp.float32)
    bg = h[:, 0:D_CONV]
    z = h[:, D_CONV:2 * D_CONV] * h[:, 2 * D_CONV:3 * D_CONV]

    @pl.when(j == 0)
    def _():
        zbuf_ref[0:8, :] = jnp.zeros((8, D_CONV), jnp.float32)

    zbuf_ref[8:8 + tm, :] = z
    z1 = zbuf_ref[7:7 + tm, :]
    z2 = zbuf_ref[6:6 + tm, :]
    y = (cw_ref[0:1, :] * z2 + cw_ref[1:2, :] * z1) + cw_ref[2:3, :] * z
    conv_ref[0] = (bg * y).astype(conv_ref.dtype)
    zbuf_ref[0:8, :] = z[tm - 8:tm, :]


def _in_proj(x1, wk, wki, wqt, wvt, wqit, wwit, wc, cw):
    b, s, d = x1.shape
    tm = SEQ_TILE
    ns = s // tm
    f32 = jnp.float32
    out_shape = (
        jax.ShapeDtypeStruct((b, s, D_ATTN), MXU_DTYPE),
        jax.ShapeDtypeStruct((b, s, LANES), MXU_DTYPE),
        jax.ShapeDtypeStruct((b, ns, N_HEADS * HEAD_PAD, tm), MXU_DTYPE),
        jax.ShapeDtypeStruct((b, ns, N_HEADS * V_PAD, tm), MXU_DTYPE),
        jax.ShapeDtypeStruct((b, ns, N_IDX_HEADS * HEAD_PAD, tm), MXU_DTYPE),
        jax.ShapeDtypeStruct((b, ns, N_IDX_HEADS, tm), f32),
        jax.ShapeDtypeStruct((b, s, D_CONV), MXU_DTYPE),
    )
    row_spec = lambda w: pl.BlockSpec((1, tm, w), lambda bi, j: (bi, j, 0))
    tile_spec = lambda r: pl.BlockSpec((1, 1, r, tm), lambda bi, j: (bi, j, 0, 0))
    return pl.pallas_call(
        _in_proj_kernel,
        grid=(b, ns),
        in_specs=[
            pl.BlockSpec((1, tm, d), lambda bi, j: (bi, j, 0)),
            _resident(wk.shape), _resident(wki.shape), _resident(wqt.shape), _resident(wvt.shape),
            _resident(wqit.shape), _resident(wwit.shape), _resident(wc.shape), _resident(cw.shape),
        ],
        out_specs=(
            row_spec(D_ATTN), row_spec(LANES), tile_spec(N_HEADS * HEAD_PAD), tile_spec(N_HEADS * V_PAD),
            tile_spec(N_IDX_HEADS * HEAD_PAD), tile_spec(N_IDX_HEADS), row_spec(D_CONV),
        ),
        out_shape=out_shape,
        scratch_shapes=[pltpu.VMEM((tm + 8, D_CONV), f32)],
        compiler_params=pltpu.CompilerParams(
            dimension_semantics=("arbitrary", "arbitrary"), vmem_limit_bytes=VMEM_LIMIT_BYTES),
        name="in_proj",
    )(x1, wk, wki, wqt, wvt, wqit, wwit, wc, cw)


def _tree_reduce(op, x):
    r, c = x.shape
    chains = [None] * REDUCE_CHAINS
    for g in range(r // 8):
        part = x[g * 8:(g + 1) * 8, :]
        k = g % REDUCE_CHAINS
        chains[k] = part if chains[k] is None else op(chains[k], part)
    out = chains[0]
    for extra in chains[1:]:
        if extra is not None:
            out = op(out, extra)
    return out


def _col_count(mask):
    return _tree_reduce(jnp.add, jnp.where(mask, 1, 0).astype(jnp.int32))


def _col_min(x):
    return _tree_reduce(jnp.minimum, x)


def _dsa_kernel(topk, qt_ref, qit_ref, wit_ref, k_ref, kidx_ref, vt_ref, pf_ref, o_ref,
                s_ref, acc_ref, rhs_ref, lga_ref, lgb_ref):
    i = pl.program_id(1)
    kt = SEQ_TILE
    qb = qt_ref.shape[3]
    f32 = jnp.float32
    inf = jnp.float32(jnp.inf)
    n_tiles = i + 1

    seq_len = s_ref.shape[0] - kt
    chunk_shift = CHUNK.bit_length() - 1
    lane_pos = lax.broadcasted_iota(jnp.int32, (1, qb), 1)
    row_pos = lax.broadcasted_iota(jnp.int32, (kt, 1), 0)
    q_pos = i * qb + lane_pos
    n_adm = ((q_pos >> chunk_shift) + 1) * CHUNK

    def tile_rows(t):
        return pl.ds(pl.multiple_of(t * kt, kt), kt)

    def score_tile(t):
        kx = kidx_ref[0, tile_rows(t), :]
        s = jnp.zeros((kt, qb), f32)
        for h in range(N_IDX_HEADS):
            rel = jnp.dot(kx, qit_ref[0, 0, h * HEAD_PAD:(h + 1) * HEAD_PAD, :],
                          preferred_element_type=f32)
            s = s + jnp.maximum(rel, 0.0) * wit_ref[0, 0, h:h + 1, :]
        return s

    def phase_a(j, carry):
        smax, smin = carry
        for t, counted in ((2 * j, None), (2 * j + 1, 2 * j + 1 < i)):
            s = score_tile(t)
            s_ref[tile_rows(t), :] = s
            tmax = jnp.max(s, axis=0, keepdims=True)
            tmin = jnp.min(s, axis=0, keepdims=True)
            if counted is not None:
                tmax = jnp.where(counted, tmax, -inf)
                tmin = jnp.where(counted, tmin, inf)
            smax = jnp.maximum(smax, tmax)
            smin = jnp.minimum(smin, tmin)
        return smax, smin

    smax, smin = lax.fori_loop(0, (i + 1) >> 1, phase_a,
                               (jnp.full((1, qb), -inf), jnp.full((1, qb), inf)))
    s = score_tile(i)
    adm = (row_pos >> chunk_shift) <= (lane_pos >> chunk_shift)
    s_ref[tile_rows(i), :] = jnp.where(adm, s, -inf)
    smax = jnp.maximum(smax, jnp.max(jnp.where(adm, s, -inf), axis=0, keepdims=True))
    smin = jnp.minimum(smin, jnp.min(jnp.where(adm, s, inf), axis=0, keepdims=True))

    s_ref[tile_rows(n_tiles), :] = jnp.full((kt, qb), -inf)
    pair_pos = lax.broadcasted_iota(jnp.int32, (2 * kt, 1), 0)

    def sweep(fn, init):
        def body(j, carry):
            start = pl.multiple_of(j * (2 * kt), 2 * kt)
            return fn(s_ref[pl.ds(start, 2 * kt), :], start, carry)
        return lax.fori_loop(0, (n_tiles + 1) >> 1, body, init)

    def count_ge(p):
        c8 = sweep(lambda tile, start, c: c + _col_count(tile >= p), jnp.zeros((8, qb), jnp.int32))
        return jnp.sum(c8, axis=0, keepdims=True)

    searching = n_adm > topk

    def bisect(_, carry):
        lo, hi, c_lo = carry
        p = lo * 0.5 + hi * 0.5
        c = count_ge(p)
        up = c >= topk
        return jnp.where(up, p, lo), jnp.where(up, hi, p), jnp.where(up, c, c_lo)

    lo, _, c_lo = lax.fori_loop(0, BISECT_ROUNDS, bisect, (smin, smax, n_adm))
    thr = jnp.min(sweep(lambda tile, start, c: jnp.minimum(c, _col_min(jnp.where(tile >= lo, tile, inf))),
                        jnp.full((8, qb), inf)), axis=0, keepdims=True)

    def above(thr):
        def fn(tile, start, carry):
            gt8, nx8 = carry
            gt = tile > thr
            return gt8 + _col_count(gt), jnp.minimum(nx8, _col_min(jnp.where(gt, tile, inf)))
        gt8, nx8 = sweep(fn, (jnp.zeros((8, qb), jnp.int32), jnp.full((8, qb), inf)))
        return jnp.sum(gt8, axis=0, keepdims=True), jnp.min(nx8, axis=0, keepdims=True)

    def settle(thr):
        def peel_cond(carry):
            _, _, n_gt, _ = carry
            return jnp.max(n_gt) >= topk

        def peel_body(carry):
            thr, n_ge, n_gt, nxt = carry
            step = n_gt >= topk
            thr = jnp.where(step, nxt, thr)
            return (thr, jnp.where(step, n_gt, n_ge)) + above(thr)

        thr, n_ge, n_gt, _ = lax.while_loop(peel_cond, peel_body, (thr, c_lo) + above(thr))
        n_ties_kept = topk - n_gt
        cut_needed = jnp.logical_and(searching, n_ge > topk)

        @pl.when(jnp.max(cut_needed.astype(jnp.int32)) > 0)
        def _():
            def ties_upto(c):
                def fn(tile, start, acc):
                    return acc + _col_count(jnp.logical_and(tile == thr, start + pair_pos <= c))
                return jnp.sum(sweep(fn, jnp.zeros((8, qb), jnp.int32)), axis=0, keepdims=True)

            def pos_bisect(_, carry):
                lo_p, hi_p = carry
                mid = (lo_p + hi_p) >> 1
                ok = ties_upto(mid) >= n_ties_kept
                return jnp.where(ok, lo_p, mid + 1), jnp.where(ok, mid, hi_p)

            _, cut = lax.fori_loop(0, (seq_len - 1).bit_length(), pos_bisect,
                                   (jnp.zeros((1, qb), jnp.int32),
                                    jnp.full((1, qb), seq_len - 1, jnp.int32)))

            def knock(tile, start, _):
                drop = jnp.logical_and(cut_needed, jnp.logical_and(tile == thr, start + pair_pos > cut))
                s_ref[pl.ds(start, 2 * kt), :] = jnp.where(drop, -inf, tile)
                return 0
            sweep(knock, 0)

        return thr

    unsettled = jnp.logical_and(searching, c_lo != topk)
    thr = lax.cond(jnp.max(unsettled.astype(jnp.int32)) > 0, settle, lambda t: t, thr)
    thr = jnp.where(searching, thr, smin)


    acc_ref[...] = jnp.zeros(acc_ref.shape, f32)
    slopes = [2.0 ** (-8.0 * (h + 1) / N_HEADS) for h in range(N_HEADS)]
    pos_row = lax.broadcasted_iota(jnp.int32, (POS_ROWS, 1), 0)
    q_hi = (q_pos >> chunk_shift).astype(f32)
    q_lo = (q_pos & (CHUNK - 1)).astype(f32)
    for h in range(N_HEADS):
        pos_rows = jnp.where(pos_row == 0, slopes[h] * CHUNK,
                             jnp.where(pos_row == 1, slopes[h],
                                       jnp.where(pos_row == 2, -slopes[h] * CHUNK * q_hi,
                                                 jnp.where(pos_row == 3, -slopes[h] * q_lo, 0.0))))
        rhs_ref[h, 0:HEAD_PAD, :] = qt_ref[0, 0, h * HEAD_PAD:(h + 1) * HEAD_PAD, :]
        rhs_ref[h, HEAD_PAD:HEAD_PAD + POS_ROWS, :] = pos_rows.astype(rhs_ref.dtype)
        rhs_ref[h, HEAD_PAD + POS_ROWS:, :] = jnp.zeros((LANES - POS_ROWS, qb), rhs_ref.dtype)

    def logits_stage(t, lg, thr_t, diagonal=False):
        head = logits_head(t, lg, thr_t, diagonal)
        return jnp.concatenate([head(h) for h in range(N_HEADS)], axis=0)

    def logits_head(t, lg, thr_t, diagonal=False):
        rows = tile_rows(t)
        mask_bias = jnp.where(s_ref[rows, :] >= thr_t, 0.0, -inf)
        pf = pf_ref[rows, :]
        if diagonal:
            twice_future = 2.0 * jnp.maximum(row_pos - lane_pos, 0).astype(f32)

        def head(h):
            lhs = jnp.concatenate([k_ref[0, rows, (h // 2) * LANES:(h // 2 + 1) * LANES], pf], axis=1)
            logit = jnp.dot(lhs, rhs_ref[h], preferred_element_type=f32) + mask_bias
            if diagonal:
                logit = logit - slopes[h] * twice_future
            lg[h] = logit
            return jnp.max(logit, axis=0, keepdims=True)
        return head

    def softmax_head(t, lg, m_all, tile_max, h):
        m_old = m_all[h:h + 1, :]
        m_new = jnp.maximum(m_old, tile_max[h:h + 1, :])
        m_safe = jnp.where(m_new == -inf, 0.0, m_new)
        alpha = jnp.exp(m_old - m_safe)
        p = jnp.exp(lg[h] - m_safe).astype(MXU_DTYPE)
        hrows = slice(h * V_PAD, (h + 1) * V_PAD)
        pv = jnp.dot(vt_ref[0, t, hrows, :], p, preferred_element_type=f32)
        acc_ref[hrows, :] = alpha * acc_ref[hrows, :] + pv
        return m_new

    def softmax_stage(t, lg, m_all, tile_max):
        return jnp.concatenate([softmax_head(t, lg, m_all, tile_max, h) for h in range(N_HEADS)], axis=0)

    def overlapped(t_soft, lg_soft, m_all, tile_max, t_next, lg_next, thr_next):
        head = logits_head(t_next, lg_next, thr_next)
        m_rows, max_rows = [], []
        for h in range(N_HEADS):
            max_rows.append(head(h))
            m_rows.append(softmax_head(t_soft, lg_soft, m_all, tile_max, h))
        return jnp.concatenate(m_rows, axis=0), jnp.concatenate(max_rows, axis=0)

    def tile_pair(j, carry):
        m_all, max_a = carry
        t0 = 2 * j
        t1 = t0 + 1
        m_all, max_b = overlapped(t0, lga_ref, m_all, max_a, t1, lgb_ref, jnp.where(t1 < i, thr, inf))
        m_all, max_a = overlapped(t1, lgb_ref, m_all, max_b, jnp.minimum(t0 + 2, i), lga_ref, thr)
        return m_all, max_a

    m_all, _ = lax.fori_loop(0, (i + 1) >> 1, tile_pair,
                             (jnp.full((N_HEADS, qb), -inf), logits_stage(0, lga_ref, thr)))
    softmax_stage(i, lga_ref, m_all, logits_stage(i, lga_ref, thr, diagonal=True))


    out = [acc_ref[h * V_PAD:h * V_PAD + HEAD_DIM, :] / acc_ref[h * V_PAD + HEAD_DIM:h * V_PAD + HEAD_DIM + 1, :]
           for h in range(N_HEADS)]
    o_ref[0] = jnp.concatenate(out, axis=0).T.astype(o_ref.dtype)


def _dsa(qt, qit, wit, k, kidx, vt, pf):
    b, ns, _, qb = qt.shape
    s = k.shape[1]
    topk = min(TOPK_MAX, s // 4)
    batch_spec = lambda shape: pl.BlockSpec((1,) + shape[1:], lambda bi, i: (bi,) + (0,) * (len(shape) - 1),
                                            pipeline_mode=pl.Buffered(1))
    tile_spec = lambda r: pl.BlockSpec((1, 1, r, qb), lambda bi, i: (bi, i, 0, 0))
    return pl.pallas_call(
        functools.partial(_dsa_kernel, topk),
        grid=(b, ns),
        in_specs=[
            tile_spec(qt.shape[2]), tile_spec(qit.shape[2]), tile_spec(wit.shape[2]),
            batch_spec(k.shape), batch_spec(kidx.shape), batch_spec(vt.shape), _resident(pf.shape),
        ],
        out_specs=pl.BlockSpec((1, qb, D_ATTN), lambda bi, i: (bi, i, 0)),
        out_shape=jax.ShapeDtypeStruct((b, s, D_ATTN), MXU_DTYPE),
        scratch_shapes=[
            pltpu.VMEM((s + SEQ_TILE, qb), jnp.float32),
            pltpu.VMEM((N_HEADS * V_PAD, qb), jnp.float32),
            pltpu.VMEM((N_HEADS, 2 * LANES, qb), MXU_DTYPE),
            pltpu.VMEM((N_HEADS, SEQ_TILE, qb), jnp.float32),
            pltpu.VMEM((N_HEADS, SEQ_TILE, qb), jnp.float32),
        ],
        compiler_params=pltpu.CompilerParams(
            dimension_semantics=("arbitrary", "arbitrary"), vmem_limit_bytes=VMEM_LIMIT_BYTES),
        name="dsa",
    )(qt, qit, wit, k, kidx, vt, pf)


def _tail_kernel(alpha, attn_ref, conv_ref, x1_ref, p_ref, woa_ref, woc_ref, g2_ref, b2_ref,
                 wg_ref, wu_ref, wd_ref, g3_ref, b3_ref, wgate_ref, wproj_ref, o_ref, act_ref):
    f32 = jnp.float32
    mix = jnp.dot(attn_ref[...], woa_ref[...], preferred_element_type=f32) \
        + jnp.dot(conv_ref[...], woc_ref[...], preferred_element_type=f32)
    x2 = _layer_norm(alpha * x1_ref[...] + mix, g2_ref[...], b2_ref[...])
    ffn = _swiglu(x2.astype(MXU_DTYPE), wg_ref, wu_ref, wd_ref, act_ref)
    x3 =_layer_norm(alpha * x2 + 0.5 * ffn, g3_ref[...], b3_ref[...])
    gate = jax.nn.sigmoid(jnp.dot(x3.astype(MXU_DTYPE), wgate_ref[...], preferred_element_type=f32))
    proj = jnp.dot(p_ref[...].astype(MXU_DTYPE), wproj_ref[...], preferred_element_type=f32)
    o_ref[...] = x3 + gate * proj


def _tail(attn, conv, x1, p, woa, woc, g2, b2, wg3, wu3, wd3, g3, b3, wgate, wproj, alpha):
    n, d = x1.shape
    tm = min(ROW_TILE, n)
    row = lambda w: pl.BlockSpec((tm, w), lambda i: (i, 0))
    weights = (woa, woc, g2, b2, wg3, wu3, wd3, g3, b3, wgate, wproj)
    return pl.pallas_call(
        functools.partial(_tail_kernel, alpha),
        grid=(n // tm,),
        in_specs=[row(attn.shape[1]), row(conv.shape[1]), row(d), row(p.shape[1])]
        + [_resident(w.shape) for w in weights],
        out_specs=row(d),
        out_shape=jax.ShapeDtypeStruct((n, d), jnp.float32),
        scratch_shapes=[pltpu.VMEM((tm, wd3.shape[0]), MXU_DTYPE)],
        compiler_params=pltpu.CompilerParams(
            dimension_semantics=("arbitrary",), vmem_limit_bytes=VMEM_LIMIT_BYTES),
        name="tail",
    )(attn, conv, x1, p, *weights)


def _ffn_weights(wg, wu, wd):
    d, f = wg.shape
    n = f // FF_CHUNK
    chunk_cols = lambda w: w.reshape(d, n, FF_CHUNK).transpose(1, 0, 2).astype(MXU_DTYPE)
    return chunk_cols(wg), chunk_cols(wu), wd.astype(MXU_DTYPE)


def _padded_head_rows(w, parity_of_head):
    d = w.shape[0]
    wt = w.T.reshape(-1, HEAD_DIM, d)
    zero = jnp.zeros((HEAD_DIM, d), w.dtype)
    groups = []
    for h in range(wt.shape[0]):
        groups += [wt[h], zero] if parity_of_head(h) == 0 else [zero, wt[h]]
    return jnp.concatenate(groups, axis=0).astype(MXU_DTYPE)


def _key_position_features(s):
    pos = jnp.arange(s, dtype=jnp.int32)[:, None]
    col = jnp.arange(LANES, dtype=jnp.int32)[None, :]
    feat = jnp.where(col == 0, pos // CHUNK, jnp.where(col == 1, pos % CHUNK, jnp.where(col < 4, 1, 0)))
    return feat.astype(MXU_DTYPE)


def kernel(x, p, ln1_g, ln1_b, ffn1_wg, ffn1_wu, ffn1_wd, w_in, conv_w, w_out, ln2_g, ln2_b,
           ffn2_wg, ffn2_wu, ffn2_wd, ln3_g, ln3_b, ple_gate_w, ple_proj_w):
    b, s, d = x.shape
    depth = p.shape[0]
    alpha = (2 * depth) ** 0.25
    assert d == D_MODEL and s % SEQ_TILE == 0 and (b * s) % min(ROW_TILE, b * s) == 0
    o_q, o_k, o_v, o_qi = 0, D_ATTN, 2 * D_ATTN, 3 * D_ATTN
    o_ki = o_qi + N_IDX_HEADS * IDX_DIM
    o_wi = o_ki + IDX_DIM
    o_conv = o_wi + N_IDX_HEADS
    row2 = lambda v: v.reshape(1, -1)

    for i in range(depth):
        wi = w_in[i]
        x1 = _ffn_ln(x.reshape(b * s, d), *_ffn_weights(ffn1_wg[i], ffn1_wu[i], ffn1_wd[i]),
                     row2(ln1_g[i]), row2(ln1_b[i]), alpha)
        wk = wi[:, o_k:o_v].astype(MXU_DTYPE)
        wki = jnp.pad(wi[:, o_ki:o_wi], ((0, 0), (0, LANES - IDX_DIM))).astype(MXU_DTYPE)
        wqt = _padded_head_rows(wi[:, o_q:o_k] * (HEAD_DIM ** -0.5), lambda h: h % 2)
        wvt = wi[:, o_v:o_qi].T.astype(MXU_DTYPE)
        wqit = _padded_head_rows(wi[:, o_qi:o_ki], lambda h: 0)
        wwit = jnp.pad(wi[:, o_wi:o_conv].T, ((0, LANES - N_IDX_HEADS), (0, 0))).astype(MXU_DTYPE)
        wc = wi[:, o_conv:].astype(MXU_DTYPE)
        k, kidx, qt, vt, qit, wit, conv = _in_proj(
            x1.reshape(b, s, d), wk, wki, wqt, wvt, wqit, wwit, wc, conv_w[i])
        attn = _dsa(qt, qit, wit, k, kidx, vt, _key_position_features(s))
        wo = w_out[i].astype(MXU_DTYPE)
        x = _tail(attn.reshape(b * s, D_ATTN), conv.reshape(b * s, D_CONV), x1, p[i].reshape(b * s, -1),
                  wo[:D_ATTN], wo[D_ATTN:], row2(ln2_g[i]), row2(ln2_b[i]),
                  *_ffn_weights(ffn2_wg[i], ffn2_wu[i], ffn2_wd[i]),
                  row2(ln3_g[i]), row2(ln3_b[i]),
                  ple_gate_w[i].astype(MXU_DTYPE), ple_proj_w[i].astype(MXU_DTYPE), alpha).reshape(b, s, d)
    return x
```

```python
import functools

import jax
import jax.numpy as jnp
from jax import lax
from jax.experimental import pallas as pl
from jax.experimental.pallas import tpu as pltpu

D_MODEL = 1024
CHUNK = 64
HEAD_DIM = 64
N_HEADS = 8
D_ATTN = N_HEADS * HEAD_DIM
N_IDX_HEADS = 8
IDX_DIM = 64
TOPK_MAX = 256
D_CONV = D_MODEL - D_ATTN
CONV_WIDTH = 3
D_FF = 2816
D_PLE = 256
LN_EPS = 1e-5
IDX_SCALE = (N_IDX_HEADS ** -0.5) * (IDX_DIM ** -0.5)
LOG2E = 1.4426950408889634

MXU_DTYPE = jnp.bfloat16

LANES = 128
FF_CHUNK = 256
ROW_TILE = 512
SEQ_TILE = 256
HEAD_PAD = 2 * HEAD_DIM
V_PAD = HEAD_DIM + 16
POS_ROWS = 16
REDUCE_CHAINS = 4
BISECT_ROUNDS = 14
VMEM_LIMIT_BYTES = 56 * 1024 * 1024


def _layer_norm(y, g, b):
    mu = jnp.mean(y, axis=-1, keepdims=True)
    yc = y - mu
    var = jnp.mean(yc * yc, axis=-1, keepdims=True)
    return yc * lax.rsqrt(var + LN_EPS) * g + b


def _swiglu(xb, wg_ref, wu_ref, wd_ref, act_ref):
    for c in range(wg_ref.shape[0]):
        g = jnp.dot(xb, wg_ref[c], preferred_element_type=jnp.float32)
        u = jnp.dot(xb, wu_ref[c], preferred_element_type=jnp.float32)
        act_ref[:, c * FF_CHUNK:(c + 1) * FF_CHUNK] = ((g * jax.nn.sigmoid(g)) * u).astype(act_ref.dtype)
    return jnp.dot(act_ref[...], wd_ref[...], preferred_element_type=jnp.float32)


def _ffn_ln_kernel(alpha, x_ref, wg_ref, wu_ref, wd_ref, g_ref, b_ref, o_ref, act_ref):
    x = x_ref[...]
    ffn = _swiglu(x.astype(MXU_DTYPE), wg_ref, wu_ref, wd_ref, act_ref)
    o_ref[...] = _layer_norm(alpha * x + 0.5 * ffn, g_ref[...], b_ref[...])


def _resident(shape):
    nd = len(shape)
    return pl.BlockSpec(shape, lambda *_: (0,) * nd, pipeline_mode=pl.Buffered(1))


def _ffn_ln(x2d, wg3, wu3, wd3, g, b, alpha):
    n, d = x2d.shape
    tm = min(ROW_TILE, n)
    return pl.pallas_call(
        functools.partial(_ffn_ln_kernel, alpha),
        grid=(n // tm,),
        in_specs=[
            pl.BlockSpec((tm, d), lambda i: (i, 0)),
            _resident(wg3.shape), _resident(wu3.shape), _resident(wd3.shape),
            _resident(g.shape), _resident(b.shape),
        ],
        out_specs=pl.BlockSpec((tm, d), lambda i: (i, 0)),
        out_shape=jax.ShapeDtypeStruct((n, d), jnp.float32),
        scratch_shapes=[pltpu.VMEM((tm, wd3.shape[0]), MXU_DTYPE)],
        compiler_params=pltpu.CompilerParams(
            dimension_semantics=("arbitrary",), vmem_limit_bytes=VMEM_LIMIT_BYTES),
        name="ffn_ln",
    )(x2d, wg3, wu3, wd3, g, b)


def _in_proj_kernel(x_ref, wk_ref, wki_ref, wqt_ref, wvt_ref, wqit_ref, wwit_ref, wc_ref, cw_ref,
                    k_ref, kidx_ref, qt_ref, vt_ref, qit_ref, wit_ref, conv_ref, zbuf_ref):
    j = pl.program_id(1)
    tm = x_ref.shape[1]
    xb = x_ref[0].astype(MXU_DTYPE)
    nt = (((1,), (1,)), ((), ()))

    k_ref[0] = jnp.dot(xb, wk_ref[...], preferred_element_type=jnp.float32).astype(k_ref.dtype)
    kidx_ref[0] = jnp.dot(xb, wki_ref[...], preferred_element_type=jnp.float32).astype(kidx_ref.dtype)
    def padded_heads(yt, upper_half_of_head):
        zero = jnp.zeros((HEAD_DIM, tm), jnp.float32)
        groups = []
        for h in range(yt.shape[0] // HEAD_DIM):
            rows = yt[h * HEAD_DIM:(h + 1) * HEAD_DIM]
            groups += [zero, rows] if upper_half_of_head(h) else [rows, zero]
        return jnp.concatenate(groups, axis=0)

    qt = lax.dot_general(wqt_ref[...], xb, nt, preferred_element_type=jnp.float32)
    qt_ref[0, 0] = padded_heads(qt, lambda h: h % 2 == 1).astype(qt_ref.dtype)
    vt =lax.dot_general(wvt_ref[...], xb, nt, preferred_element_type=jnp.float32)
    ones = jnp.ones((V_PAD - HEAD_DIM, tm), jnp.float32)
    vt_ref[0, 0] = jnp.concatenate(
        [piece for h in range(N_HEADS) for piece in (vt[h * HEAD_DIM:(h + 1) * HEAD_DIM], ones)],
        axis=0).astype(vt_ref.dtype)
    qit = lax.dot_general(wqit_ref[...], xb, nt, preferred_element_type=jnp.float32)
    qit_ref[0, 0] = padded_heads(qit, lambda h: False).astype(qit_ref.dtype)
    wit = lax.dot_general(wwit_ref[...], xb, nt, preferred_element_type=jnp.float32)
    wit_ref[0, 0] = wit[0:N_IDX_HEADS, :] * IDX_SCALE

    h = jnp.dot(xb, wc_ref[...], preferred_element_type=jnp.float32)
    bg = h[:, 0:D_CONV]
    z = h[:, D_CONV:2 * D_CONV] * h[:, 2 * D_CONV:3 * D_CONV]

    @pl.when(j == 0)
    def _():
        zbuf_ref[0:8, :] = jnp.zeros((8, D_CONV), jnp.float32)

    zbuf_ref[8:8 + tm, :] = z
    z1 = zbuf_ref[7:7 + tm, :]
    z2 = zbuf_ref[6:6 + tm, :]
    y = (cw_ref[0:1, :] * z2 + cw_ref[1:2, :] * z1) + cw_ref[2:3, :] * z
    conv_ref[0] = (bg * y).astype(conv_ref.dtype)
    zbuf_ref[0:8, :] = z[tm - 8:tm, :]


def _in_proj(x1, wk, wki, wqt, wvt, wqit, wwit, wc, cw):
    b, s, d = x1.shape
    tm = SEQ_TILE
    ns = s // tm
    f32 = jnp.float32
    out_shape = (
        jax.ShapeDtypeStruct((b, s, D_ATTN), MXU_DTYPE),
        jax.ShapeDtypeStruct((b, s, LANES), MXU_DTYPE),
        jax.ShapeDtypeStruct((b, ns, N_HEADS * HEAD_PAD, tm), MXU_DTYPE),
        jax.ShapeDtypeStruct((b, ns, N_HEADS * V_PAD, tm), MXU_DTYPE),
        jax.ShapeDtypeStruct((b, ns, N_IDX_HEADS * HEAD_PAD, tm), MXU_DTYPE),
        jax.ShapeDtypeStruct((b, ns, N_IDX_HEADS, tm), f32),
        jax.ShapeDtypeStruct((b, s, D_CONV), MXU_DTYPE),
    )
    row_spec = lambda w: pl.BlockSpec((1, tm, w), lambda bi, j: (bi, j, 0))
    tile_spec = lambda r: pl.BlockSpec((1, 1, r, tm), lambda bi, j: (bi, j, 0, 0))
    return pl.pallas_call(
        _in_proj_kernel,
        grid=(b, ns),
        in_specs=[
            pl.BlockSpec((1, tm, d), lambda bi, j: (bi, j, 0)),
            _resident(wk.shape), _resident(wki.shape), _resident(wqt.shape), _resident(wvt.shape),
            _resident(wqit.shape), _resident(wwit.shape), _resident(wc.shape), _resident(cw.shape),
        ],
        out_specs=(
            row_spec(D_ATTN), row_spec(LANES), tile_spec(N_HEADS * HEAD_PAD), tile_spec(N_HEADS * V_PAD),
            tile_spec(N_IDX_HEADS * HEAD_PAD), tile_spec(N_IDX_HEADS), row_spec(D_CONV),
        ),
        out_shape=out_shape,
        scratch_shapes=[pltpu.VMEM((tm + 8, D_CONV), f32)],
        compiler_params=pltpu.CompilerParams(
            dimension_semantics=("arbitrary", "arbitrary"), vmem_limit_bytes=VMEM_LIMIT_BYTES),
        name="in_proj",
    )(x1, wk, wki, wqt, wvt, wqit, wwit, wc, cw)


def _tree_reduce(op, x):
    r, c = x.shape
    chains = [None] * REDUCE_CHAINS
    for g in range(r // 8):
        part = x[g * 8:(g + 1) * 8, :]
        k = g % REDUCE_CHAINS
        chains[k] = part if chains[k] is None else op(chains[k], part)
    out = chains[0]
    for extra in chains[1:]:
        if extra is not None:
            out = op(out, extra)
    return out


def _col_count(mask):
    return _tree_reduce(jnp.add, jnp.where(mask, 1, 0).astype(jnp.int32))


def _col_min(x):
    return _tree_reduce(jnp.minimum, x)


def _dsa_kernel(topk, qt_ref, qit_ref, wit_ref, k_ref, kidx_ref, vt_ref, pf_ref, o_ref,
                s_ref, acc_ref, rhs_ref, lga_ref, lgb_ref):
    i = pl.program_id(1)
    kt = SEQ_TILE
    qb = qt_ref.shape[3]
    f32 = jnp.float32
    inf = jnp.float32(jnp.inf)
    n_tiles = i + 1

    seq_len = s_ref.shape[0] - kt
    chunk_shift = CHUNK.bit_length() - 1
    lane_pos = lax.broadcasted_iota(jnp.int32, (1, qb), 1)
    row_pos = lax.broadcasted_iota(jnp.int32, (kt, 1), 0)
    q_pos = i * qb + lane_pos
    n_adm = ((q_pos >> chunk_shift) + 1) * CHUNK

    def tile_rows(t):
        return pl.ds(pl.multiple_of(t * kt, kt), kt)

    def score_tile(t):
        kx = kidx_ref[0, tile_rows(t), :]
        s = jnp.zeros((kt, qb), f32)
        for h in range(N_IDX_HEADS):
            rel = jnp.dot(kx, qit_ref[0, 0, h * HEAD_PAD:(h + 1) * HEAD_PAD, :],
                          preferred_element_type=f32)
            s = s + jnp.maximum(rel, 0.0) * wit_ref[0, 0, h:h + 1, :]
        return s

    def phase_a(j, carry):
        smax, smin = carry
        for t, counted in ((2 * j, None), (2 * j + 1, 2 * j + 1 < i)):
            s = score_tile(t)
            s_ref[tile_rows(t), :] = s
            tmax = jnp.max(s, axis=0, keepdims=True)
            tmin = jnp.min(s, axis=0, keepdims=True)
            if counted is not None:
                tmax = jnp.where(counted, tmax, -inf)
                tmin = jnp.where(counted, tmin, inf)
            smax = jnp.maximum(smax, tmax)
            smin = jnp.minimum(smin, tmin)
        return smax, smin

    smax, smin = lax.fori_loop(0, (i + 1) >> 1, phase_a,
                               (jnp.full((1, qb), -inf), jnp.full((1, qb), inf)))
    s = score_tile(i)
    adm = (row_pos >> chunk_shift) <= (lane_pos >> chunk_shift)
    s_ref[tile_rows(i), :] = jnp.where(adm, s, -inf)
    smax = jnp.maximum(smax, jnp.max(jnp.where(adm, s, -inf), axis=0, keepdims=True))
    smin = jnp.minimum(smin, jnp.min(jnp.where(adm, s, inf), axis=0, keepdims=True))

    s_ref[tile_rows(n_tiles), :] = jnp.full((kt, qb), -inf)
    pair_pos = lax.broadcasted_iota(jnp.int32, (2 * kt, 1), 0)

    def sweep(fn, init):
        def body(j, carry):
            start = pl.multiple_of(j * (2 * kt), 2 * kt)
            return fn(s_ref[pl.ds(start, 2 * kt), :], start, carry)
        return lax.fori_loop(0, (n_tiles + 1) >> 1, body, init)

    def count_ge(p):
        c8 = sweep(lambda tile, start, c: c + _col_count(tile >= p), jnp.zeros((8, qb), jnp.int32))
        return jnp.sum(c8, axis=0, keepdims=True)

    searching = n_adm > topk

    def bisect(_, carry):
        lo, hi, c_lo = carry
        p = lo * 0.5 + hi * 0.5
        c = count_ge(p)
        up = c >= topk
        return jnp.where(up, p, lo), jnp.where(up, hi, p), jnp.where(up, c, c_lo)

    lo, _, c_lo = lax.fori_loop(0, BISECT_ROUNDS, bisect, (smin, smax, n_adm))
    thr = jnp.min(sweep(lambda tile, start, c: jnp.minimum(c, _col_min(jnp.where(tile >= lo, tile, inf))),
                        jnp.full((8, qb), inf)), axis=0, keepdims=True)

    def above(thr):
        def fn(tile, start, carry):
            gt8, nx8 = carry
            gt = tile > thr
            return gt8 + _col_count(gt), jnp.minimum(nx8, _col_min(jnp.where(gt, tile, inf)))
        gt8, nx8 = sweep(fn, (jnp.zeros((8, qb), jnp.int32), jnp.full((8, qb), inf)))
        return jnp.sum(gt8, axis=0, keepdims=True), jnp.min(nx8, axis=0, keepdims=True)

    def settle(thr):
        def peel_cond(carry):
            _, _, n_gt, _ = carry
            return jnp.max(n_gt) >= topk

        def peel_body(carry):
            thr, n_ge, n_gt, nxt = carry
            step = n_gt >= topk
            thr = jnp.where(step, nxt, thr)
            return (thr, jnp.where(step, n_gt, n_ge)) + above(thr)

        thr, n_ge, n_gt, _ = lax.while_loop(peel_cond, peel_body, (thr, c_lo) + above(thr))
        n_ties_kept = topk - n_gt
        cut_needed = jnp.logical_and(searching, n_ge > topk)

        @pl.when(jnp.max(cut_needed.astype(jnp.int32)) > 0)
        def _():
            def ties_upto(c):
                def fn(tile, start, acc):
                    return acc + _col_count(jnp.logical_and(tile == thr, start + pair_pos <= c))
                return jnp.sum(sweep(fn, jnp.zeros((8, qb), jnp.int32)), axis=0, keepdims=True)

            def pos_bisect(_, carry):
                lo_p, hi_p = carry
                mid = (lo_p + hi_p) >> 1
                ok = ties_upto(mid) >= n_ties_kept
                return jnp.where(ok, lo_p, mid + 1), jnp.where(ok, mid, hi_p)

            _, cut = lax.fori_loop(0, (seq_len - 1).bit_length(), pos_bisect,
                                   (jnp.zeros((1, qb), jnp.int32),
                                    jnp.full((1, qb), seq_len - 1, jnp.int32)))

            def knock(tile, start, _):
                drop = jnp.logical_and(cut_needed, jnp.logical_and(tile == thr, start + pair_pos > cut))
                s_ref[pl.ds(start, 2 * kt), :] = jnp.where(drop, -inf, tile)
                return 0
            sweep(knock, 0)

        return thr

    unsettled = jnp.logical_and(searching, c_lo != topk)
    thr = lax.cond(jnp.max(unsettled.astype(jnp.int32)) > 0, settle, lambda t: t, thr)
    thr = jnp.where(searching, thr, smin)


    acc_ref[...] = jnp.zeros(acc_ref.shape, f32)
    slopes = [LOG2E * 2.0 ** (-8.0 * (h + 1) / N_HEADS) for h in range(N_HEADS)]
    q_posf = q_pos.astype(f32)

    def three_pieces(x):
        hi = x.astype(MXU_DTYPE).astype(f32)
        mid = (x - hi).astype(MXU_DTYPE).astype(f32)
        return [hi, mid, x - hi - mid]

    for h in range(N_HEADS):
        pieces = (three_pieces(jnp.full((1, qb), slopes[h] * CHUNK, f32))
                  + three_pieces(jnp.full((1, qb), slopes[h], f32))
                  + three_pieces(-slopes[h] * q_posf))
        pos_rows = jnp.concatenate(pieces + [jnp.zeros((POS_ROWS - len(pieces), qb), f32)], axis=0)
        rhs_ref[h, 0:HEAD_PAD, :] = qt_ref[0, 0, h * HEAD_PAD:(h + 1) * HEAD_PAD, :]
        rhs_ref[h, HEAD_PAD:HEAD_PAD + POS_ROWS, :] = pos_rows.astype(rhs_ref.dtype)
        rhs_ref[h, HEAD_PAD + POS_ROWS:, :] = jnp.zeros((LANES - POS_ROWS, qb), rhs_ref.dtype)

    def logits_stage(t, lg, thr_t, diagonal=False):
        head = logits_head(t, lg, thr_t, diagonal)
        return jnp.concatenate([head(h) for h in range(N_HEADS)], axis=0)

    def logits_head(t, lg, thr_t, diagonal=False):
        rows = tile_rows(t)
        mask_bias = jnp.where(s_ref[rows, :] >= thr_t, 0.0, -inf)
        pf = pf_ref[rows, :]
        if diagonal:
            twice_future = 2.0 * jnp.maximum(row_pos - lane_pos, 0).astype(f32)

        def head(h):
            lhs = jnp.concatenate([k_ref[0, rows, (h // 2) * LANES:(h // 2 + 1) * LANES], pf], axis=1)
            logit = jnp.dot(lhs, rhs_ref[h], preferred_element_type=f32) + mask_bias
            if diagonal:
                logit = logit - slopes[h] * twice_future
            lg[h] = logit
            return jnp.max(logit, axis=0, keepdims=True)
        return head

    def softmax_head(t, lg, m_all, tile_max, h):
        m_old = m_all[h:h + 1, :]
        m_new = jnp.maximum(m_old, tile_max[h:h + 1, :])
        m_safe = jnp.where(m_new == -inf, 0.0, m_new)
        alpha = jnp.exp2(m_old - m_safe)
        p = jnp.exp2(lg[h] - m_safe).astype(MXU_DTYPE)
        hrows = slice(h * V_PAD, (h + 1) * V_PAD)
        pv = jnp.dot(vt_ref[0, t, hrows, :], p, preferred_element_type=f32)
        acc_ref[hrows, :] = alpha * acc_ref[hrows, :] + pv
        return m_new

    def softmax_stage(t, lg, m_all, tile_max):
        return jnp.concatenate([softmax_head(t, lg, m_all, tile_max, h) for h in range(N_HEADS)], axis=0)

    def overlapped(t_soft, lg_soft, m_all, tile_max, t_next, lg_next, thr_next):
        head = logits_head(t_next, lg_next, thr_next)
        m_rows, max_rows = [], []
        for h in range(N_HEADS):
            max_rows.append(head(h))
            m_rows.append(softmax_head(t_soft, lg_soft, m_all, tile_max, h))
        return jnp.concatenate(m_rows, axis=0), jnp.concatenate(max_rows, axis=0)

    def tile_pair(j, carry):
        m_all, max_a = carry
        t0 = 2 * j
        t1 = t0 + 1
        m_all, max_b = overlapped(t0, lga_ref, m_all, max_a, t1, lgb_ref, jnp.where(t1 < i, thr, inf))
        m_all, max_a = overlapped(t1, lgb_ref, m_all, max_b, jnp.minimum(t0 + 2, i), lga_ref, thr)
        return m_all, max_a

    m_all, _ = lax.fori_loop(0, (i + 1) >> 1, tile_pair,
                             (jnp.full((N_HEADS, qb), -inf), logits_stage(0, lga_ref, thr)))
    softmax_stage(i, lga_ref, m_all, logits_stage(i, lga_ref, thr, diagonal=True))


    out = [acc_ref[h * V_PAD:h * V_PAD + HEAD_DIM, :] / acc_ref[h * V_PAD + HEAD_DIM:h * V_PAD + HEAD_DIM + 1, :]
           for h in range(N_HEADS)]
    o_ref[0] = jnp.concatenate(out, axis=0).T.astype(o_ref.dtype)


def _dsa(qt, qit, wit, k, kidx, vt, pf):
    b, ns, _, qb = qt.shape
    s = k.shape[1]
    topk = min(TOPK_MAX, s // 4)
    batch_spec = lambda shape: pl.BlockSpec((1,) + shape[1:], lambda bi, i: (bi,) + (0,) * (len(shape) - 1),
                                            pipeline_mode=pl.Buffered(1))
    tile_spec = lambda r: pl.BlockSpec((1, 1, r, qb), lambda bi, i: (bi, i, 0, 0))
    return pl.pallas_call(
        functools.partial(_dsa_kernel, topk),
        grid=(b, ns),
        in_specs=[
            tile_spec(qt.shape[2]), tile_spec(qit.shape[2]), tile_spec(wit.shape[2]),
            batch_spec(k.shape), batch_spec(kidx.shape), batch_spec(vt.shape), _resident(pf.shape),
        ],
        out_specs=pl.BlockSpec((1, qb, D_ATTN), lambda bi, i: (bi, i, 0)),
        out_shape=jax.ShapeDtypeStruct((b, s, D_ATTN), MXU_DTYPE),
        scratch_shapes=[
            pltpu.VMEM((s + SEQ_TILE, qb), jnp.float32),
            pltpu.VMEM((N_HEADS * V_PAD, qb), jnp.float32),
            pltpu.VMEM((N_HEADS, 2 * LANES, qb), MXU_DTYPE),
            pltpu.VMEM((N_HEADS, SEQ_TILE, qb), jnp.float32),
            pltpu.VMEM((N_HEADS, SEQ_TILE, qb), jnp.float32),
        ],
        compiler_params=pltpu.CompilerParams(
            dimension_semantics=("arbitrary", "arbitrary"), vmem_limit_bytes=VMEM_LIMIT_BYTES),
        name="dsa",
    )(qt, qit, wit, k, kidx, vt, pf)


def _tail_kernel(alpha, attn_ref, conv_ref, x1_ref, p_ref, woa_ref, woc_ref, g2_ref, b2_ref,
                 wg_ref, wu_ref, wd_ref, g3_ref, b3_ref, wgate_ref, wproj_ref, o_ref, act_ref):
    f32 = jnp.float32
    mix = jnp.dot(attn_ref[...], woa_ref[...], preferred_element_type=f32) \
        + jnp.dot(conv_ref[...], woc_ref[...], preferred_element_type=f32)
    x2 = _layer_norm(alpha * x1_ref[...] + mix, g2_ref[...], b2_ref[...])
    ffn = _swiglu(x2.astype(MXU_DTYPE), wg_ref, wu_ref, wd_ref, act_ref)
    x3 =_layer_norm(alpha * x2 + 0.5 * ffn, g3_ref[...], b3_ref[...])
    gate = jax.nn.sigmoid(jnp.dot(x3.astype(MXU_DTYPE), wgate_ref[...], preferred_element_type=f32))
    proj = jnp.dot(p_ref[...].astype(MXU_DTYPE), wproj_ref[...], preferred_element_type=f32)
    o_ref[...] = x3 + gate * proj


def _tail(attn, conv, x1, p, woa, woc, g2, b2, wg3, wu3, wd3, g3, b3, wgate, wproj, alpha):
    n, d = x1.shape
    tm = min(ROW_TILE, n)
    row = lambda w: pl.BlockSpec((tm, w), lambda i: (i, 0))
    weights = (woa, woc, g2, b2, wg3, wu3, wd3, g3, b3, wgate, wproj)
    return pl.pallas_call(
        functools.partial(_tail_kernel, alpha),
        grid=(n // tm,),
        in_specs=[row(attn.shape[1]), row(conv.shape[1]), row(d), row(p.shape[1])]
        + [_resident(w.shape) for w in weights],
        out_specs=row(d),
        out_shape=jax.ShapeDtypeStruct((n, d), jnp.float32),
        scratch_shapes=[pltpu.VMEM((tm, wd3.shape[0]), MXU_DTYPE)],
        compiler_params=pltpu.CompilerParams(
            dimension_semantics=("arbitrary",), vmem_limit_bytes=VMEM_LIMIT_BYTES),
        name="tail",
    )(attn, conv, x1, p, *weights)


def _ffn_weights(wg, wu, wd):
    d, f = wg.shape
    n = f // FF_CHUNK
    chunk_cols = lambda w: w.reshape(d, n, FF_CHUNK).transpose(1, 0, 2).astype(MXU_DTYPE)
    return chunk_cols(wg), chunk_cols(wu), wd.astype(MXU_DTYPE)


def _key_position_features(s):
    pos = jnp.arange(s, dtype=jnp.int32)[:, None]
    col = jnp.arange(LANES, dtype=jnp.int32)[None, :]
    feat = jnp.where(col < 3, pos // CHUNK, jnp.where(col < 6, pos % CHUNK, jnp.where(col < 9, 1, 0)))
    return feat.astype(MXU_DTYPE)


def kernel(x, p, ln1_g, ln1_b, ffn1_wg, ffn1_wu, ffn1_wd, w_in, conv_w, w_out, ln2_g, ln2_b,
           ffn2_wg, ffn2_wu, ffn2_wd, ln3_g, ln3_b, ple_gate_w, ple_proj_w):
    b, s, d = x.shape
    depth = p.shape[0]
    alpha = (2 * depth) ** 0.25
    assert d == D_MODEL and s % SEQ_TILE == 0 and (b * s) % min(ROW_TILE, b * s) == 0
    o_q, o_k, o_v, o_qi = 0, D_ATTN, 2 * D_ATTN, 3 * D_ATTN
    o_ki = o_qi + N_IDX_HEADS * IDX_DIM
    o_wi = o_ki + IDX_DIM
    o_conv = o_wi + N_IDX_HEADS
    row2 = lambda v: v.reshape(1, -1)

    for i in range(depth):
        wi = w_in[i]
        x1 = _ffn_ln(x.reshape(b * s, d), *_ffn_weights(ffn1_wg[i], ffn1_wu[i], ffn1_wd[i]),
                     row2(ln1_g[i]), row2(ln1_b[i]), alpha)
        wk = wi[:, o_k:o_v].astype(MXU_DTYPE)
        wki = jnp.pad(wi[:, o_ki:o_wi], ((0, 0), (0, LANES - IDX_DIM))).astype(MXU_DTYPE)
        wqt = (wi[:, o_q:o_k] * (LOG2E * HEAD_DIM ** -0.5)).T.astype(MXU_DTYPE)
        wvt = wi[:, o_v:o_qi].T.astype(MXU_DTYPE)
        wqit = wi[:, o_qi:o_ki].T.astype(MXU_DTYPE)
        wwit = jnp.pad(wi[:, o_wi:o_conv].T, ((0, LANES - N_IDX_HEADS), (0, 0))).astype(MXU_DTYPE)
        wc = wi[:, o_conv:].astype(MXU_DTYPE)
        k, kidx, qt, vt, qit, wit, conv = _in_proj(
            x1.reshape(b, s, d), wk, wki, wqt, wvt, wqit, wwit, wc, conv_w[i])
        attn = _dsa(qt, qit, wit, k, kidx, vt, _key_position_features(s))
        wo = w_out[i].astype(MXU_DTYPE)
        x = _tail(attn.reshape(b * s, D_ATTN), conv.reshape(b * s, D_CONV), x1, p[i].reshape(b * s, -1),
                  wo[:D_ATTN], wo[D_ATTN:], row2(ln2_g[i]), row2(ln2_b[i]),
                  *_ffn_weights(ffn2_wg[i], ffn2_wu[i], ffn2_wd[i]),
                  row2(ln3_g[i]), row2(ln3_b[i]),
                  ple_gate_w[i].astype(MXU_DTYPE), ple_proj_w[i].astype(MXU_DTYPE), alpha).reshape(b, s, d)
    return x
```

```python
import functools

import jax
import jax.numpy as jnp
from jax import lax
from jax.experimental import pallas as pl
from jax.experimental.pallas import tpu as pltpu

D_MODEL = 1024
CHUNK = 64
HEAD_DIM = 64
N_HEADS = 8
D_ATTN = N_HEADS * HEAD_DIM
N_IDX_HEADS = 8
IDX_DIM = 64
TOPK_MAX = 256
D_CONV = D_MODEL - D_ATTN
CONV_WIDTH = 3
D_FF = 2816
D_PLE = 256
LN_EPS = 1e-5
IDX_SCALE = (N_IDX_HEADS ** -0.5) * (IDX_DIM ** -0.5)
LOG2E = 1.4426950408889634

MXU_DTYPE = jnp.bfloat16

LANES = 128
FF_CHUNK = 256
ROW_TILE = 512
SEQ_TILE = 256
HEAD_PAD = 2 * HEAD_DIM
V_PAD = HEAD_DIM + 16
POS_ROWS = 16
REDUCE_CHAINS = 4
BISECT_ROUNDS = 14
VMEM_LIMIT_BYTES = 56 * 1024 * 1024


def _layer_norm(y, g, b):
    mu = jnp.mean(y, axis=-1, keepdims=True)
    yc = y - mu
    var = jnp.mean(yc * yc, axis=-1, keepdims=True)
    return yc * lax.rsqrt(var + LN_EPS) * g + b


def _swiglu(xb, wg_ref, wu_ref, wd_ref, act_ref):
    for c in range(wg_ref.shape[1] // FF_CHUNK):
        cols = slice(c * FF_CHUNK, (c + 1) * FF_CHUNK)
        g = jnp.dot(xb, wg_ref[:, cols], preferred_element_type=jnp.float32)
        u = jnp.dot(xb, wu_ref[:, cols], preferred_element_type=jnp.float32)
        act_ref[:, cols] = ((g * jax.nn.sigmoid(g)) * u).astype(act_ref.dtype)
    return jnp.dot(act_ref[...], wd_ref[...], preferred_element_type=jnp.float32)


def _ffn_ln_kernel(alpha, x_ref, wg_ref, wu_ref, wd_ref, g_ref, b_ref, o_ref, act_ref):
    x = x_ref[...]
    ffn = _swiglu(x.astype(MXU_DTYPE), wg_ref, wu_ref, wd_ref, act_ref)
    o_ref[...] = _layer_norm(alpha * x + 0.5 * ffn, g_ref[...], b_ref[...])


def _resident(shape):
    nd = len(shape)
    return pl.BlockSpec(shape, lambda *_: (0,) * nd, pipeline_mode=pl.Buffered(1))


def _ffn_ln(x2d, wg3, wu3, wd3, g, b, alpha):
    n, d = x2d.shape
    tm = min(ROW_TILE, n)
    return pl.pallas_call(
        functools.partial(_ffn_ln_kernel, alpha),
        grid=(n // tm,),
        in_specs=[
            pl.BlockSpec((tm, d), lambda i: (i, 0)),
            _resident(wg3.shape), _resident(wu3.shape), _resident(wd3.shape),
            _resident(g.shape), _resident(b.shape),
        ],
        out_specs=pl.BlockSpec((tm, d), lambda i: (i, 0)),
        out_shape=jax.ShapeDtypeStruct((n, d), jnp.float32),
        scratch_shapes=[pltpu.VMEM((tm, wd3.shape[0]), MXU_DTYPE)],
        compiler_params=pltpu.CompilerParams(
            dimension_semantics=("arbitrary",), vmem_limit_bytes=VMEM_LIMIT_BYTES),
        name="ffn_ln",
    )(x2d, wg3, wu3, wd3, g, b)


def _in_proj_kernel(x_ref, wk_ref, wki_ref, wqt_ref, wvt_ref, wqit_ref, wwit_ref, wc_ref, cw_ref,
                    k_ref, kidx_ref, qt_ref, vt_ref, qit_ref, wit_ref, conv_ref, zbuf_ref):
    j = pl.program_id(1)
    tm = x_ref.shape[1]
    xb = x_ref[0].astype(MXU_DTYPE)
    nt = (((1,), (1,)), ((), ()))

    k_ref[0] = jnp.dot(xb, wk_ref[...], preferred_element_type=jnp.float32).astype(k_ref.dtype)
    kidx_ref[0] = jnp.dot(xb, wki_ref[...], preferred_element_type=jnp.float32).astype(kidx_ref.dtype)
    def padded_heads(yt, upper_half_of_head):
        zero = jnp.zeros((HEAD_DIM, tm), jnp.float32)
        groups = []
        for h in range(yt.shape[0] // HEAD_DIM):
            rows = yt[h * HEAD_DIM:(h + 1) * HEAD_DIM]
            groups += [zero, rows] if upper_half_of_head(h) else [rows, zero]
        return jnp.concatenate(groups, axis=0)

    qt = lax.dot_general(wqt_ref[...], xb, nt, preferred_element_type=jnp.float32)
    qt_ref[0, 0] = padded_heads(qt, lambda h: h % 2 == 1).astype(qt_ref.dtype)
    vt =lax.dot_general(wvt_ref[...], xb, nt, preferred_element_type=jnp.float32)
    ones = jnp.ones((V_PAD - HEAD_DIM, tm), jnp.float32)
    vt_ref[0, 0] = jnp.concatenate(
        [piece for h in range(N_HEADS) for piece in (vt[h * HEAD_DIM:(h + 1) * HEAD_DIM], ones)],
        axis=0).astype(vt_ref.dtype)
    qit = lax.dot_general(wqit_ref[...], xb, nt, preferred_element_type=jnp.float32)
    qit_ref[0, 0] = padded_heads(qit, lambda h: False).astype(qit_ref.dtype)
    wit = lax.dot_general(wwit_ref[...], xb, nt, preferred_element_type=jnp.float32)
    wit_ref[0, 0] = wit[0:N_IDX_HEADS, :] * IDX_SCALE

    h = jnp.dot(xb, wc_ref[...], preferred_element_type=jnp.float32)
    bg = h[:, 0:D_CONV]
    z = h[:, D_CONV:2 * D_CONV] * h[:, 2 * D_CONV:3 * D_CONV]

    @pl.when(j == 0)
    def _():
        zbuf_ref[0:8, :] = jnp.zeros((8, D_CONV), jnp.float32)

    zbuf_ref[8:8 + tm, :] = z
    z1 = zbuf_ref[7:7 + tm, :]
    z2 = zbuf_ref[6:6 + tm, :]
    y = (cw_ref[0:1, :] * z2 + cw_ref[1:2, :] * z1) + cw_ref[2:3, :] * z
    conv_ref[0] = (bg * y).astype(conv_ref.dtype)
    zbuf_ref[0:8, :] = z[tm - 8:tm, :]


def _in_proj(x1, wk, wki, wqt, wvt, wqit, wwit, wc, cw):
    b, s, d = x1.shape
    tm = SEQ_TILE
    ns = s // tm
    f32 = jnp.float32
    out_shape = (
        jax.ShapeDtypeStruct((b, s, D_ATTN), MXU_DTYPE),
        jax.ShapeDtypeStruct((b, s, LANES), MXU_DTYPE),
        jax.ShapeDtypeStruct((b, ns, N_HEADS * HEAD_PAD, tm), MXU_DTYPE),
        jax.ShapeDtypeStruct((b, ns, N_HEADS * V_PAD, tm), MXU_DTYPE),
        jax.ShapeDtypeStruct((b, ns, N_IDX_HEADS * HEAD_PAD, tm), MXU_DTYPE),
        jax.ShapeDtypeStruct((b, ns, N_IDX_HEADS, tm), f32),
        jax.ShapeDtypeStruct((b, s, D_CONV), MXU_DTYPE),
    )
    row_spec = lambda w: pl.BlockSpec((1, tm, w), lambda bi, j: (bi, j, 0))
    tile_spec = lambda r: pl.BlockSpec((1, 1, r, tm), lambda bi, j: (bi, j, 0, 0))
    return pl.pallas_call(
        _in_proj_kernel,
        grid=(b, ns),
        in_specs=[
            pl.BlockSpec((1, tm, d), lambda bi, j: (bi, j, 0)),
            _resident(wk.shape), _resident(wki.shape), _resident(wqt.shape), _resident(wvt.shape),
            _resident(wqit.shape), _resident(wwit.shape), _resident(wc.shape), _resident(cw.shape),
        ],
        out_specs=(
            row_spec(D_ATTN), row_spec(LANES), tile_spec(N_HEADS * HEAD_PAD), tile_spec(N_HEADS * V_PAD),
            tile_spec(N_IDX_HEADS * HEAD_PAD), tile_spec(N_IDX_HEADS), row_spec(D_CONV),
        ),
        out_shape=out_shape,
        scratch_shapes=[pltpu.VMEM((tm + 8, D_CONV), f32)],
        compiler_params=pltpu.CompilerParams(
            dimension_semantics=("arbitrary", "arbitrary"), vmem_limit_bytes=VMEM_LIMIT_BYTES),
        name="in_proj",
    )(x1, wk, wki, wqt, wvt, wqit, wwit, wc, cw)


def _tree_reduce(op, x):
    r, c = x.shape
    chains = [None] * REDUCE_CHAINS
    for g in range(r // 8):
        part = x[g * 8:(g + 1) * 8, :]
        k = g % REDUCE_CHAINS
        chains[k] = part if chains[k] is None else op(chains[k], part)
    out = chains[0]
    for extra in chains[1:]:
        if extra is not None:
            out = op(out, extra)
    return out


def _col_count(mask):
    return _tree_reduce(jnp.add, jnp.where(mask, 1, 0).astype(jnp.int32))


def _col_min(x):
    return _tree_reduce(jnp.minimum, x)


def _dsa_kernel(topk, qt_ref, qit_ref, wit_ref, k_ref, kidx_ref, vt_ref, pf_ref, o_ref,
                s_ref, acc_ref, rhs_ref, lg_ref):
    i = pl.program_id(1)
    kt = SEQ_TILE
    qb = qt_ref.shape[3]
    f32 = jnp.float32
    inf = jnp.float32(jnp.inf)
    n_tiles = i + 1

    seq_len = s_ref.shape[0] - kt
    chunk_shift = CHUNK.bit_length() - 1
    lane_pos = lax.broadcasted_iota(jnp.int32, (1, qb), 1)
    row_pos = lax.broadcasted_iota(jnp.int32, (kt, 1), 0)
    q_pos = i * qb + lane_pos
    n_adm = ((q_pos >> chunk_shift) + 1) * CHUNK

    def tile_rows(t):
        return pl.ds(pl.multiple_of(t * kt, kt), kt)

    def score_rows(rows, n_rows):
        kx = kidx_ref[0, rows, :]
        s = jnp.zeros((n_rows, qb), f32)
        for h in range(N_IDX_HEADS):
            rel = jnp.dot(kx, qit_ref[0, 0, h * HEAD_PAD:(h + 1) * HEAD_PAD, :],
                          preferred_element_type=f32)
            s = s + jnp.maximum(rel, 0.0) * wit_ref[0, 0, h:h + 1, :]
        return s

    def score_tile(t):
        return score_rows(tile_rows(t), kt)

    def phase_a(j, carry):
        smax, smin = carry
        rows = pl.ds(pl.multiple_of(j * (2 * kt), 2 * kt), 2 * kt)
        s = score_rows(rows, 2 * kt)
        s_ref[rows, :] = s
        second_counted = 2 * j + 1 < i
        for half, counted in ((s[0:kt], None), (s[kt:2 * kt], second_counted)):
            tmax = jnp.max(half, axis=0, keepdims=True)
            tmin = jnp.min(half, axis=0, keepdims=True)
            if counted is not None:
                tmax = jnp.where(counted, tmax, -inf)
                tmin = jnp.where(counted, tmin, inf)
            smax = jnp.maximum(smax, tmax)
            smin = jnp.minimum(smin, tmin)
        return smax, smin

    smax, smin = lax.fori_loop(0, (i + 1) >> 1, phase_a,
                               (jnp.full((1, qb), -inf), jnp.full((1, qb), inf)))
    s = score_tile(i)
    adm = (row_pos >> chunk_shift) <= (lane_pos >> chunk_shift)
    s_ref[tile_rows(i), :] = jnp.where(adm, s, -inf)
    smax = jnp.maximum(smax, jnp.max(jnp.where(adm, s, -inf), axis=0, keepdims=True))
    smin = jnp.minimum(smin, jnp.min(jnp.where(adm, s, inf), axis=0, keepdims=True))

    s_ref[tile_rows(n_tiles), :] = jnp.full((kt, qb), -inf)
    pair_pos = lax.broadcasted_iota(jnp.int32, (2 * kt, 1), 0)

    def sweep(fn, init):
        def body(j, carry):
            start = pl.multiple_of(j * (2 * kt), 2 * kt)
            return fn(s_ref[pl.ds(start, 2 * kt), :], start, carry)
        return lax.fori_loop(0, (n_tiles + 1) >> 1, body, init)

    def count_ge(p):
        c8 = sweep(lambda tile, start, c: c + _col_count(tile >= p), jnp.zeros((8, qb), jnp.int32))
        return jnp.sum(c8, axis=0, keepdims=True)

    searching = n_adm > topk

    def bisect(_, carry):
        lo, hi, c_lo = carry
        p = lo * 0.5 + hi * 0.5
        c = count_ge(p)
        up = c >= topk
        return jnp.where(up, p, lo), jnp.where(up, hi, p), jnp.where(up, c, c_lo)

    lo, _, c_lo = lax.fori_loop(0, BISECT_ROUNDS, bisect, (smin, smax, n_adm))
    thr = jnp.min(sweep(lambda tile, start, c: jnp.minimum(c, _col_min(jnp.where(tile >= lo, tile, inf))),
                        jnp.full((8, qb), inf)), axis=0, keepdims=True)

    def above(thr):
        def fn(tile, start, carry):
            gt8, nx8 = carry
            gt = tile > thr
            return gt8 + _col_count(gt), jnp.minimum(nx8, _col_min(jnp.where(gt, tile, inf)))
        gt8, nx8 = sweep(fn, (jnp.zeros((8, qb), jnp.int32), jnp.full((8, qb), inf)))
        return jnp.sum(gt8, axis=0, keepdims=True), jnp.min(nx8, axis=0, keepdims=True)

    def settle(thr):
        def peel_cond(carry):
            _, _, n_gt, _ = carry
            return jnp.max(n_gt) >= topk

        def peel_body(carry):
            thr, n_ge, n_gt, nxt = carry
            step = n_gt >= topk
            thr = jnp.where(step, nxt, thr)
            return (thr, jnp.where(step, n_gt, n_ge)) + above(thr)

        thr, n_ge, n_gt, _ = lax.while_loop(peel_cond, peel_body, (thr, c_lo) + above(thr))
        n_ties_kept = topk - n_gt
        cut_needed = jnp.logical_and(searching, n_ge > topk)

        @pl.when(jnp.max(cut_needed.astype(jnp.int32)) > 0)
        def _():
            def ties_upto(c):
                def fn(tile, start, acc):
                    return acc + _col_count(jnp.logical_and(tile == thr, start + pair_pos <= c))
                return jnp.sum(sweep(fn, jnp.zeros((8, qb), jnp.int32)), axis=0, keepdims=True)

            def pos_bisect(_, carry):
                lo_p, hi_p = carry
                mid = (lo_p + hi_p) >> 1
                ok = ties_upto(mid) >= n_ties_kept
                return jnp.where(ok, lo_p, mid + 1), jnp.where(ok, mid, hi_p)

            _, cut = lax.fori_loop(0, (seq_len - 1).bit_length(), pos_bisect,
                                   (jnp.zeros((1, qb), jnp.int32),
                                    jnp.full((1, qb), seq_len - 1, jnp.int32)))

            def knock(tile, start, _):
                drop = jnp.logical_and(cut_needed, jnp.logical_and(tile == thr, start + pair_pos > cut))
                s_ref[pl.ds(start, 2 * kt), :] = jnp.where(drop, -inf, tile)
                return 0
            sweep(knock, 0)

        return thr

    unsettled = jnp.logical_and(searching, c_lo != topk)
    thr = lax.cond(jnp.max(unsettled.astype(jnp.int32)) > 0, settle, lambda t: t, thr)
    thr = jnp.where(searching, thr, smin)


    acc_ref[...] = jnp.zeros(acc_ref.shape, f32)
    slopes = [LOG2E * 2.0 ** (-8.0 * (h + 1) / N_HEADS) for h in range(N_HEADS)]
    q_posf = q_pos.astype(f32)

    def three_pieces(x):
        hi = x.astype(MXU_DTYPE).astype(f32)
        mid = (x - hi).astype(MXU_DTYPE).astype(f32)
        return [hi, mid, x - hi - mid]

    for h in range(N_HEADS):
        pieces = (three_pieces(jnp.full((1, qb), slopes[h] * CHUNK, f32))
                  + three_pieces(jnp.full((1, qb), slopes[h], f32))
                  + three_pieces(-slopes[h] * q_posf))
        pos_rows = jnp.concatenate(pieces + [jnp.zeros((POS_ROWS - len(pieces), qb), f32)], axis=0)
        rhs_ref[h, 0:HEAD_PAD, :] = qt_ref[0, 0, h * HEAD_PAD:(h + 1) * HEAD_PAD, :]
        rhs_ref[h, HEAD_PAD:HEAD_PAD + POS_ROWS, :] = pos_rows.astype(rhs_ref.dtype)
        rhs_ref[h, HEAD_PAD + POS_ROWS:, :] = jnp.zeros((LANES - POS_ROWS, qb), rhs_ref.dtype)

    def logits_head(t, n_tiles_in, thr_list, diagonal=False):
        n_rows = n_tiles_in * kt
        rows = pl.ds(pl.multiple_of(t * kt, kt), n_rows)
        scores = s_ref[rows, :]
        mask_bias = jnp.concatenate(
            [jnp.where(scores[r * kt:(r + 1) * kt] >= thr_list[r], 0.0, -inf) for r in range(n_tiles_in)], axis=0)
        pf = pf_ref[rows, :]
        if diagonal:
            twice_future = 2.0 * jnp.maximum(row_pos - lane_pos, 0).astype(f32)

        def head(h):
            lhs = jnp.concatenate([k_ref[0, rows, (h // 2) * LANES:(h // 2 + 1) * LANES], pf], axis=1)
            logit = jnp.dot(lhs, rhs_ref[h], preferred_element_type=f32) + mask_bias
            if diagonal:
                logit = logit - slopes[h] * twice_future
            lg_ref[h, 0:n_rows, :] = logit
            return [jnp.max(logit[r * kt:(r + 1) * kt], axis=0, keepdims=True) for r in range(n_tiles_in)]
        return head

    def softmax_head(t, slot, m_old, tile_max, h):
        m_new = jnp.maximum(m_old, tile_max)
        m_safe = jnp.where(m_new == -inf, 0.0, m_new)
        alpha = jnp.exp2(m_old - m_safe)
        p = jnp.exp2(lg_ref[h, slot * kt:(slot + 1) * kt, :] - m_safe).astype(MXU_DTYPE)
        hrows = slice(h * V_PAD, (h + 1) * V_PAD)
        pv = jnp.dot(vt_ref[0, t, hrows, :], p, preferred_element_type=f32)
        acc_ref[hrows, :] = alpha * acc_ref[hrows, :] + pv
        return m_new

    def rows_of(x):
        return [x[h:h + 1, :] for h in range(N_HEADS)]

    def tile_pair(j, carry):
        m_all, max0, max1 = carry
        t0 = 2 * j
        t_next = jnp.minimum(t0 + 2, jnp.maximum(i - 1, 0))
        head = logits_head(t_next, 2, [thr, jnp.where(t0 + 3 < i, thr, inf)])
        m_rows, max0_rows, max1_rows = [], [], []
        for h, (m_h, a_h, b_h) in enumerate(zip(rows_of(m_all), rows_of(max0), rows_of(max1))):
            m_h = softmax_head(t0, 0, m_h, a_h, h)
            m_h = softmax_head(t0 + 1, 1, m_h, b_h, h)
            a_next, b_next = head(h)
            m_rows.append(m_h)
            max0_rows.append(a_next)
            max1_rows.append(b_next)
        return tuple(jnp.concatenate(r, axis=0) for r in (m_rows, max0_rows, max1_rows))

    first = logits_head(0, 2, [thr, jnp.where(1 < i, thr, inf)])
    first_max = [first(h) for h in range(N_HEADS)]
    m_all, _, _ = lax.fori_loop(
        0, (i + 1) >> 1, tile_pair,
        (jnp.full((N_HEADS, qb), -inf),
         jnp.concatenate([a for a, _ in first_max], axis=0), jnp.concatenate([b for _, b in first_max], axis=0)))
    last = logits_head(i, 1, [thr], diagonal=True)
    for h, m_h in enumerate(rows_of(m_all)):
        softmax_head(i, 0, m_h, last(h)[0], h)


    out = [acc_ref[h * V_PAD:h * V_PAD + HEAD_DIM, :] / acc_ref[h * V_PAD + HEAD_DIM:h * V_PAD + HEAD_DIM + 1, :]
           for h in range(N_HEADS)]
    o_ref[0] = jnp.concatenate(out, axis=0).T.astype(o_ref.dtype)


def _dsa(qt, qit, wit, k, kidx, vt, pf):
    b, ns, _, qb = qt.shape
    s = k.shape[1]
    topk = min(TOPK_MAX, s // 4)
    batch_spec = lambda shape: pl.BlockSpec((1,) + shape[1:], lambda bi, i: (bi,) + (0,) * (len(shape) - 1),
                                            pipeline_mode=pl.Buffered(1))
    tile_spec = lambda r: pl.BlockSpec((1, 1, r, qb), lambda bi, i: (bi, i, 0, 0))
    return pl.pallas_call(
        functools.partial(_dsa_kernel, topk),
        grid=(b, ns),
        in_specs=[
            tile_spec(qt.shape[2]), tile_spec(qit.shape[2]), tile_spec(wit.shape[2]),
            batch_spec(k.shape), batch_spec(kidx.shape), batch_spec(vt.shape), _resident(pf.shape),
        ],
        out_specs=pl.BlockSpec((1, qb, D_ATTN), lambda bi, i: (bi, i, 0)),
        out_shape=jax.ShapeDtypeStruct((b, s, D_ATTN), MXU_DTYPE),
        scratch_shapes=[
            pltpu.VMEM((s + SEQ_TILE, qb), jnp.float32),
            pltpu.VMEM((N_HEADS * V_PAD, qb), jnp.float32),
            pltpu.VMEM((N_HEADS, 2 * LANES, qb), MXU_DTYPE),
            pltpu.VMEM((N_HEADS, 2 * SEQ_TILE, qb), jnp.float32),
        ],
        compiler_params=pltpu.CompilerParams(
            dimension_semantics=("arbitrary", "arbitrary"), vmem_limit_bytes=VMEM_LIMIT_BYTES),
        name="dsa",
    )(qt, qit, wit, k, kidx, vt, pf)


def _tail_kernel(alpha, attn_ref, conv_ref, x1_ref, p_ref, woa_ref, woc_ref, g2_ref, b2_ref,
                 wg_ref, wu_ref, wd_ref, g3_ref, b3_ref, wgate_ref, wproj_ref, o_ref, act_ref):
    f32 = jnp.float32
    mix = jnp.dot(attn_ref[...], woa_ref[...], preferred_element_type=f32) \
        + jnp.dot(conv_ref[...], woc_ref[...], preferred_element_type=f32)
    x2 = _layer_norm(alpha * x1_ref[...] + mix, g2_ref[...], b2_ref[...])
    ffn = _swiglu(x2.astype(MXU_DTYPE), wg_ref, wu_ref, wd_ref, act_ref)
    x3 =_layer_norm(alpha * x2 + 0.5 * ffn, g3_ref[...], b3_ref[...])
    gate = jax.nn.sigmoid(jnp.dot(x3.astype(MXU_DTYPE), wgate_ref[...], preferred_element_type=f32))
    proj = jnp.dot(p_ref[...].astype(MXU_DTYPE), wproj_ref[...], preferred_element_type=f32)
    o_ref[...] = x3 + gate * proj


def _tail(attn, conv, x1, p, woa, woc, g2, b2, wg3, wu3, wd3, g3, b3, wgate, wproj, alpha):
    n, d = x1.shape
    tm = min(ROW_TILE, n)
    row = lambda w: pl.BlockSpec((tm, w), lambda i: (i, 0))
    weights = (woa, woc, g2, b2, wg3, wu3, wd3, g3, b3, wgate, wproj)
    return pl.pallas_call(
        functools.partial(_tail_kernel, alpha),
        grid=(n // tm,),
        in_specs=[row(attn.shape[1]), row(conv.shape[1]), row(d), row(p.shape[1])]
        + [_resident(w.shape) for w in weights],
        out_specs=row(d),
        out_shape=jax.ShapeDtypeStruct((n, d), jnp.float32),
        scratch_shapes=[pltpu.VMEM((tm, wd3.shape[0]), MXU_DTYPE)],
        compiler_params=pltpu.CompilerParams(
            dimension_semantics=("arbitrary",), vmem_limit_bytes=VMEM_LIMIT_BYTES),
        name="tail",
    )(attn, conv, x1, p, *weights)


def _ffn_weights(wg, wu, wd):
    assert wg.shape[1] % FF_CHUNK == 0
    return wg.astype(MXU_DTYPE), wu.astype(MXU_DTYPE), wd.astype(MXU_DTYPE)


def _key_position_features(s):
    pos = jnp.arange(s, dtype=jnp.int32)[:, None]
    col = jnp.arange(LANES, dtype=jnp.int32)[None, :]
    feat = jnp.where(col < 3, pos // CHUNK, jnp.where(col < 6, pos % CHUNK, jnp.where(col < 9, 1, 0)))
    return feat.astype(MXU_DTYPE)


def kernel(x, p, ln1_g, ln1_b, ffn1_wg, ffn1_wu, ffn1_wd, w_in, conv_w, w_out, ln2_g, ln2_b,
           ffn2_wg, ffn2_wu, ffn2_wd, ln3_g, ln3_b, ple_gate_w, ple_proj_w):
    b, s, d = x.shape
    depth = p.shape[0]
    alpha = (2 * depth) ** 0.25
    assert d == D_MODEL and s % SEQ_TILE == 0 and (b * s) % min(ROW_TILE, b * s) == 0
    o_q, o_k, o_v, o_qi = 0, D_ATTN, 2 * D_ATTN, 3 * D_ATTN
    o_ki = o_qi + N_IDX_HEADS * IDX_DIM
    o_wi = o_ki + IDX_DIM
    o_conv = o_wi + N_IDX_HEADS
    row2 = lambda v: v.reshape(1, -1)

    for i in range(depth):
        wi = w_in[i]
        x1 = _ffn_ln(x.reshape(b * s, d), *_ffn_weights(ffn1_wg[i], ffn1_wu[i], ffn1_wd[i]),
                     row2(ln1_g[i]), row2(ln1_b[i]), alpha)
        wk = wi[:, o_k:o_v].astype(MXU_DTYPE)
        wki = jnp.pad(wi[:, o_ki:o_wi], ((0, 0), (0, LANES - IDX_DIM))).astype(MXU_DTYPE)
        wqt = (wi[:, o_q:o_k] * (LOG2E * HEAD_DIM ** -0.5)).T.astype(MXU_DTYPE)
        wvt = wi[:, o_v:o_qi].T.astype(MXU_DTYPE)
        wqit = wi[:, o_qi:o_ki].T.astype(MXU_DTYPE)
        wwit = jnp.pad(wi[:, o_wi:o_conv].T, ((0, LANES - N_IDX_HEADS), (0, 0))).astype(MXU_DTYPE)
        wc = wi[:, o_conv:].astype(MXU_DTYPE)
        k, kidx, qt, vt, qit, wit, conv = _in_proj(
            x1.reshape(b, s, d), wk, wki, wqt, wvt, wqit, wwit, wc, conv_w[i])
        attn = _dsa(qt, qit, wit, k, kidx, vt, _key_position_features(s))
        wo = w_out[i].astype(MXU_DTYPE)
        x = _tail(attn.reshape(b * s, D_ATTN), conv.reshape(b * s, D_CONV), x1, p[i].reshape(b * s, -1),
                  wo[:D_ATTN], wo[D_ATTN:], row2(ln2_g[i]), row2(ln2_b[i]),
                  *_ffn_weights(ffn2_wg[i], ffn2_wu[i], ffn2_wd[i]),
                  row2(ln3_g[i]), row2(ln3_b[i]),
                  ple_gate_w[i].astype(MXU_DTYPE), ple_proj_w[i].astype(MXU_DTYPE), alpha).reshape(b, s, d)
    return x
```

```python
import functools

import jax
import jax.numpy as jnp
from jax import lax
from jax.experimental import pallas as pl
from jax.experimental.pallas import tpu as pltpu

D_MODEL = 1024
CHUNK = 64
HEAD_DIM = 64
N_HEADS = 8
D_ATTN = N_HEADS * HEAD_DIM
N_IDX_HEADS = 8
IDX_DIM = 64
TOPK_MAX = 256
D_CONV = D_MODEL - D_ATTN
LN_EPS = 1e-5
IDX_SCALE = (N_IDX_HEADS ** -0.5) * (IDX_DIM ** -0.5)
LOG2E = 1.4426950408889634

MXU_DTYPE = jnp.bfloat16

LANES = 128
FF_CHUNK = 256
ROW_TILE = 512
SEQ_TILE = 256
HEAD_PAD = 2 * HEAD_DIM
V_PAD = HEAD_DIM + 16
POS_ROWS = 16
REDUCE_CHAINS = 4
BISECT_ROUNDS = 14
VMEM_LIMIT_BYTES = 56 * 1024 * 1024


def _layer_norm(y, g, b):
    mu = jnp.mean(y, axis=-1, keepdims=True)
    yc = y - mu
    var = jnp.mean(yc * yc, axis=-1, keepdims=True)
    return yc * lax.rsqrt(var + LN_EPS) * g + b


def _swiglu(xb, wg_ref, wu_ref, wd_ref, act_ref):
    for c in range(wg_ref.shape[1] // FF_CHUNK):
        cols = slice(c * FF_CHUNK, (c + 1) * FF_CHUNK)
        g = jnp.dot(xb, wg_ref[:, cols], preferred_element_type=jnp.float32)
        u = jnp.dot(xb, wu_ref[:, cols], preferred_element_type=jnp.float32)
        act_ref[:, cols] = ((g * jax.nn.sigmoid(g)) * u).astype(act_ref.dtype)
    return jnp.dot(act_ref[...], wd_ref[...], preferred_element_type=jnp.float32)


def _ffn_ln_kernel(alpha, x_ref, wg_ref, wu_ref, wd_ref, g_ref, b_ref, o_ref, act_ref):
    x = x_ref[...]
    ffn = _swiglu(x.astype(MXU_DTYPE), wg_ref, wu_ref, wd_ref, act_ref)
    o_ref[...] = _layer_norm(alpha * x + 0.5 * ffn, g_ref[...], b_ref[...])


def _resident(shape):
    nd = len(shape)
    return pl.BlockSpec(shape, lambda *_: (0,) * nd, pipeline_mode=pl.Buffered(1))


def _ffn_ln(x2d, wg3, wu3, wd3, g, b, alpha):
    n, d = x2d.shape
    tm = min(ROW_TILE, n)
    return pl.pallas_call(
        functools.partial(_ffn_ln_kernel, alpha),
        grid=(n // tm,),
        in_specs=[
            pl.BlockSpec((tm, d), lambda i: (i, 0)),
            _resident(wg3.shape), _resident(wu3.shape), _resident(wd3.shape),
            _resident(g.shape), _resident(b.shape),
        ],
        out_specs=pl.BlockSpec((tm, d), lambda i: (i, 0)),
        out_shape=jax.ShapeDtypeStruct((n, d), jnp.float32),
        scratch_shapes=[pltpu.VMEM((tm, wd3.shape[0]), MXU_DTYPE)],
        compiler_params=pltpu.CompilerParams(
            dimension_semantics=("arbitrary",), vmem_limit_bytes=VMEM_LIMIT_BYTES),
        name="ffn_ln",
    )(x2d, wg3, wu3, wd3, g, b)


def _in_proj_kernel(x_ref, wk_ref, wki_ref, wqt_ref, wvt_ref, wqit_ref, wwit_ref, wc_ref, cw_ref,
                    k_ref, kidx_ref, qt_ref, vt_ref, qit_ref, wit_ref, conv_ref, zbuf_ref):
    j = pl.program_id(1)
    tm = x_ref.shape[1]
    xb = x_ref[0].astype(MXU_DTYPE)
    nt = (((1,), (1,)), ((), ()))

    k_ref[0] = jnp.dot(xb, wk_ref[...], preferred_element_type=jnp.float32).astype(k_ref.dtype)
    kidx_ref[0] = jnp.dot(xb, wki_ref[...], preferred_element_type=jnp.float32).astype(kidx_ref.dtype)
    def padded_heads(yt, upper_half_of_head):
        zero = jnp.zeros((HEAD_DIM, tm), jnp.float32)
        groups = []
        for h in range(yt.shape[0] // HEAD_DIM):
            rows = yt[h * HEAD_DIM:(h + 1) * HEAD_DIM]
            groups += [zero, rows] if upper_half_of_head(h) else [rows, zero]
        return jnp.concatenate(groups, axis=0)

    qt = lax.dot_general(wqt_ref[...], xb, nt, preferred_element_type=jnp.float32)
    qt_ref[0, 0] = padded_heads(qt, lambda h: h % 2 == 1).astype(qt_ref.dtype)
    vt =lax.dot_general(wvt_ref[...], xb, nt, preferred_element_type=jnp.float32)
    ones = jnp.ones((V_PAD - HEAD_DIM, tm), jnp.float32)
    vt_ref[0, 0] = jnp.concatenate(
        [piece for h in range(N_HEADS) for piece in (vt[h * HEAD_DIM:(h + 1) * HEAD_DIM], ones)],
        axis=0).astype(vt_ref.dtype)
    qit = lax.dot_general(wqit_ref[...], xb, nt, preferred_element_type=jnp.float32)
    qit_ref[0, 0] = padded_heads(qit, lambda h: False).astype(qit_ref.dtype)
    wit = lax.dot_general(wwit_ref[...], xb, nt, preferred_element_type=jnp.float32)
    wit_ref[0, 0] = wit[0:N_IDX_HEADS, :] * IDX_SCALE

    h = jnp.dot(xb, wc_ref[...], preferred_element_type=jnp.float32)
    bg = h[:, 0:D_CONV]
    z = h[:, D_CONV:2 * D_CONV] * h[:, 2 * D_CONV:3 * D_CONV]

    @pl.when(j == 0)
    def _():
        zbuf_ref[0:8, :] = jnp.zeros((8, D_CONV), jnp.float32)

    zbuf_ref[8:8 + tm, :] = z
    z1 = zbuf_ref[7:7 + tm, :]
    z2 = zbuf_ref[6:6 + tm, :]
    y = (cw_ref[0:1, :] * z2 + cw_ref[1:2, :] * z1) + cw_ref[2:3, :] * z
    conv_ref[0] = (bg * y).astype(conv_ref.dtype)
    zbuf_ref[0:8, :] = z[tm - 8:tm, :]


def _in_proj(x1, wk, wki, wqt, wvt, wqit, wwit, wc, cw):
    b, s, d = x1.shape
    tm = SEQ_TILE
    ns = s // tm
    f32 = jnp.float32
    out_shape = (
        jax.ShapeDtypeStruct((b, s, D_ATTN), MXU_DTYPE),
        jax.ShapeDtypeStruct((b, s, LANES), MXU_DTYPE),
        jax.ShapeDtypeStruct((b, ns, N_HEADS * HEAD_PAD, tm), MXU_DTYPE),
        jax.ShapeDtypeStruct((b, ns, N_HEADS * V_PAD, tm), MXU_DTYPE),
        jax.ShapeDtypeStruct((b, ns, N_IDX_HEADS * HEAD_PAD, tm), MXU_DTYPE),
        jax.ShapeDtypeStruct((b, ns, N_IDX_HEADS, tm), f32),
        jax.ShapeDtypeStruct((b, s, D_CONV), MXU_DTYPE),
    )
    row_spec = lambda w: pl.BlockSpec((1, tm, w), lambda bi, j: (bi, j, 0))
    tile_spec = lambda r: pl.BlockSpec((1, 1, r, tm), lambda bi, j: (bi, j, 0, 0))
    return pl.pallas_call(
        _in_proj_kernel,
        grid=(b, ns),
        in_specs=[
            pl.BlockSpec((1, tm, d), lambda bi, j: (bi, j, 0)),
            _resident(wk.shape), _resident(wki.shape), _resident(wqt.shape), _resident(wvt.shape),
            _resident(wqit.shape), _resident(wwit.shape), _resident(wc.shape), _resident(cw.shape),
        ],
        out_specs=(
            row_spec(D_ATTN), row_spec(LANES), tile_spec(N_HEADS * HEAD_PAD), tile_spec(N_HEADS * V_PAD),
            tile_spec(N_IDX_HEADS * HEAD_PAD), tile_spec(N_IDX_HEADS), row_spec(D_CONV),
        ),
        out_shape=out_shape,
        scratch_shapes=[pltpu.VMEM((tm + 8, D_CONV), f32)],
        compiler_params=pltpu.CompilerParams(
            dimension_semantics=("arbitrary", "arbitrary"), vmem_limit_bytes=VMEM_LIMIT_BYTES),
        name="in_proj",
    )(x1, wk, wki, wqt, wvt, wqit, wwit, wc, cw)


def _col_reduce(op, x):
    r, c = x.shape
    chains = [None] * REDUCE_CHAINS
    for g in range(r // 8):
        part = x[g * 8:(g + 1) * 8, :]
        k = g % REDUCE_CHAINS
        chains[k] = part if chains[k] is None else op(chains[k], part)
    out = chains[0]
    for extra in chains[1:]:
        if extra is not None:
            out = op(out, extra)
    return out


def _col_count(mask):
    return _col_reduce(jnp.add, jnp.where(mask, 1, 0).astype(jnp.int32))


def _col_min(x):
    return _col_reduce(jnp.minimum, x)


def _dsa_kernel(topk, qt_ref, qit_ref, wit_ref, k_ref, kidx_ref, vt_ref, pf_ref, o_ref,
                s_ref, acc_ref, rhs_ref, lg_ref):
    i = pl.program_id(1)
    kt = SEQ_TILE
    qb = qt_ref.shape[3]
    f32 = jnp.float32
    inf = jnp.float32(jnp.inf)
    n_tiles = i + 1

    chunk_shift = CHUNK.bit_length() - 1
    lane_pos = lax.broadcasted_iota(jnp.int32, (1, qb), 1)
    row_pos = lax.broadcasted_iota(jnp.int32, (kt, 1), 0)
    q_pos = i * qb + lane_pos
    n_adm = ((q_pos >> chunk_shift) + 1) * CHUNK

    def tile_rows(t):
        return pl.ds(pl.multiple_of(t * kt, kt), kt)

    def score_rows(rows, n_rows):
        kx = kidx_ref[0, rows, :]
        s = jnp.zeros((n_rows, qb), f32)
        for h in range(N_IDX_HEADS):
            rel = jnp.dot(kx, qit_ref[0, 0, h * HEAD_PAD:(h + 1) * HEAD_PAD, :],
                          preferred_element_type=f32)
            s = s + jnp.maximum(rel, 0.0) * wit_ref[0, 0, h:h + 1, :]
        return s

    def score_tile(t):
        return score_rows(tile_rows(t), kt)

    def phase_a(j, carry):
        smax, smin = carry
        rows = pl.ds(pl.multiple_of(j * (2 * kt), 2 * kt), 2 * kt)
        s = score_rows(rows, 2 * kt)
        s_ref[rows, :] = s
        second_counted = 2 * j + 1 < i
        for half, counted in ((s[0:kt], None), (s[kt:2 * kt], second_counted)):
            tmax = jnp.max(half, axis=0, keepdims=True)
            tmin = jnp.min(half, axis=0, keepdims=True)
            if counted is not None:
                tmax = jnp.where(counted, tmax, -inf)
                tmin = jnp.where(counted, tmin, inf)
            smax = jnp.maximum(smax, tmax)
            smin = jnp.minimum(smin, tmin)
        return smax, smin

    smax, smin = lax.fori_loop(0, (i + 1) >> 1, phase_a,
                               (jnp.full((1, qb), -inf), jnp.full((1, qb), inf)))
    s = score_tile(i)
    adm = (row_pos >> chunk_shift) <= (lane_pos >> chunk_shift)
    s_ref[tile_rows(i), :] = jnp.where(adm, s, -inf)
    smax = jnp.maximum(smax, jnp.max(jnp.where(adm, s, -inf), axis=0, keepdims=True))
    smin = jnp.minimum(smin, jnp.min(jnp.where(adm, s, inf), axis=0, keepdims=True))

    s_ref[tile_rows(n_tiles), :] = jnp.full((kt, qb), -inf)

    def sweep(fn, init):
        def body(j, carry):
            start = pl.multiple_of(j * (2 * kt), 2 * kt)
            return fn(s_ref[pl.ds(start, 2 * kt), :], start, carry)
        return lax.fori_loop(0, (n_tiles + 1) >> 1, body, init)

    def count_ge(p):
        c8 = sweep(lambda tile, start, c: c + _col_count(tile >= p), jnp.zeros((8, qb), jnp.int32))
        return jnp.sum(c8, axis=0, keepdims=True)

    searching = n_adm > topk

    def bisect(_, carry):
        lo, hi, c_lo = carry
        p = lo * 0.5 + hi * 0.5
        c = count_ge(p)
        up = c >= topk
        return jnp.where(up, p, lo), jnp.where(up, hi, p), jnp.where(up, c, c_lo)

    lo, _, c_lo = lax.fori_loop(0, BISECT_ROUNDS, bisect, (smin, smax, n_adm))
    thr = jnp.min(sweep(lambda tile, start, c: jnp.minimum(c, _col_min(jnp.where(tile >= lo, tile, inf))),
                        jnp.full((8, qb), inf)), axis=0, keepdims=True)

    def above(thr):
        def fn(tile, start, carry):
            gt8, nx8 = carry
            gt = tile > thr
            return gt8 + _col_count(gt), jnp.minimum(nx8, _col_min(jnp.where(gt, tile, inf)))
        gt8, nx8 = sweep(fn, (jnp.zeros((8, qb), jnp.int32), jnp.full((8, qb), inf)))
        return jnp.sum(gt8, axis=0, keepdims=True), jnp.min(nx8, axis=0, keepdims=True)

    def settle(thr):
        def peel_cond(carry):
            _, _, n_gt, _ = carry
            return jnp.max(n_gt) >= topk

        def peel_body(carry):
            thr, n_ge, n_gt, nxt = carry
            step = n_gt >= topk
            thr = jnp.where(step, nxt, thr)
            return (thr, jnp.where(step, n_gt, n_ge)) + above(thr)

        thr, n_ge, n_gt, _ = lax.while_loop(peel_cond, peel_body, (thr, c_lo) + above(thr))
        n_ties_kept = topk - n_gt
        cut_needed = jnp.logical_and(searching, n_ge > topk)

        @pl.when(jnp.max(cut_needed.astype(jnp.int32)) > 0)
        def _():
            r_i = lax.broadcasted_iota(jnp.int32, (2 * kt, 2 * kt), 0)
            c_i = lax.broadcasted_iota(jnp.int32, (2 * kt, 2 * kt), 1)
            prefix_op = jnp.where(c_i <= r_i, 1.0, 0.0).astype(MXU_DTYPE)
            keep = jnp.where(cut_needed, n_ties_kept.astype(f32), inf)

            def knock(tile, start, seen):
                tie = tile == thr
                rank = jnp.dot(prefix_op, jnp.where(tie, 1.0, 0.0).astype(MXU_DTYPE),
                               preferred_element_type=f32) + seen
                s_ref[pl.ds(start, 2 * kt), :] = jnp.where(jnp.logical_and(tie, rank > keep), -inf, tile)
                return rank[2 * kt - 1:2 * kt, :]
            sweep(knock, jnp.zeros((1, qb), f32))

        return thr

    unsettled = jnp.logical_and(searching, c_lo != topk)
    thr = lax.cond(jnp.max(unsettled.astype(jnp.int32)) > 0, settle, lambda t: t, thr)
    thr = jnp.where(searching, thr, smin)


    acc_ref[...] = jnp.zeros(acc_ref.shape, f32)
    slopes = [LOG2E * 2.0 ** (-8.0 * (h + 1) / N_HEADS) for h in range(N_HEADS)]
    q_posf = q_pos.astype(f32)

    def three_pieces(x):
        hi = x.astype(MXU_DTYPE).astype(f32)
        mid = (x - hi).astype(MXU_DTYPE).astype(f32)
        return [hi, mid, x - hi - mid]

    for h in range(N_HEADS):
        pieces = (three_pieces(jnp.full((1, qb), slopes[h] * CHUNK, f32))
                  + three_pieces(jnp.full((1, qb), slopes[h], f32))
                  + three_pieces(-slopes[h] * q_posf))
        pos_rows = jnp.concatenate(pieces + [jnp.zeros((POS_ROWS - len(pieces), qb), f32)], axis=0)
        rhs_ref[h, 0:HEAD_PAD, :] = qt_ref[0, 0, h * HEAD_PAD:(h + 1) * HEAD_PAD, :]
        rhs_ref[h, HEAD_PAD:HEAD_PAD + POS_ROWS, :] = pos_rows.astype(rhs_ref.dtype)
        rhs_ref[h, HEAD_PAD + POS_ROWS:, :] = jnp.zeros((LANES - POS_ROWS, qb), rhs_ref.dtype)

    def logits_head(t, slot, thr_t, diagonal=False):
        rows = tile_rows(t)
        mask_bias = jnp.where(s_ref[rows, :] >= thr_t, 0.0, -inf)
        pf = pf_ref[rows, :]
        if diagonal:
            twice_future = 2.0 * jnp.maximum(row_pos - lane_pos, 0).astype(f32)

        def head(h):
            lhs = jnp.concatenate([k_ref[0, rows, (h // 2) * LANES:(h // 2 + 1) * LANES], pf], axis=1)
            logit = jnp.dot(lhs, rhs_ref[h], preferred_element_type=f32) + mask_bias
            if diagonal:
                logit = logit - slopes[h] * twice_future
            lg_ref[h, slot * kt:(slot + 1) * kt, :] = logit
            return jnp.max(logit, axis=0, keepdims=True)
        return head

    def softmax_head(t, slot, m_old, tile_max, h):
        m_new = jnp.maximum(m_old, tile_max)
        m_safe = jnp.where(m_new == -inf, 0.0, m_new)
        alpha = jnp.exp2(m_old - m_safe)
        p = jnp.exp2(lg_ref[h, slot * kt:(slot + 1) * kt, :] - m_safe).astype(MXU_DTYPE)
        hrows = slice(h * V_PAD, (h + 1) * V_PAD)
        pv = jnp.dot(vt_ref[0, t, hrows, :], p, preferred_element_type=f32)
        acc_ref[hrows, :] = alpha * acc_ref[hrows, :] + pv
        return m_new

    def rows_of(x):
        return [x[h:h + 1, :] for h in range(N_HEADS)]

    def overlapped(t_soft, slot_soft, m_all, tile_max, t_next, slot_next, thr_next):
        head = logits_head(t_next, slot_next, thr_next)
        m_rows, max_rows = [], []
        for h, (m_h, max_h) in enumerate(zip(rows_of(m_all), rows_of(tile_max))):
            max_rows.append(head(h))
            m_rows.append(softmax_head(t_soft, slot_soft, m_h, max_h, h))
        return jnp.concatenate(m_rows, axis=0), jnp.concatenate(max_rows, axis=0)

    def tile_pair(j, carry):
        m_all, max_a = carry
        t0 = 2 * j
        t1 = t0 + 1
        m_all, max_b = overlapped(t0, 0, m_all, max_a, t1, 1, jnp.where(t1 < i, thr, inf))
        m_all, max_a = overlapped(t1, 1, m_all, max_b, jnp.minimum(t0 + 2, i), 0, thr)
        return m_all, max_a

    first = logits_head(0, 0, thr)
    m_all, _ = lax.fori_loop(
        0, (i + 1) >> 1, tile_pair,
        (jnp.full((N_HEADS, qb), -inf), jnp.concatenate([first(h) for h in range(N_HEADS)], axis=0)))
    last = logits_head(i, 0, thr, diagonal=True)
    for h, m_h in enumerate(rows_of(m_all)):
        softmax_head(i, 0, m_h, last(h), h)


    out = [acc_ref[h * V_PAD:h * V_PAD + HEAD_DIM, :] / acc_ref[h * V_PAD + HEAD_DIM:h * V_PAD + HEAD_DIM + 1, :]
           for h in range(N_HEADS)]
    o_ref[0] = jnp.concatenate(out, axis=0).T.astype(o_ref.dtype)


def _dsa(qt, qit, wit, k, kidx, vt, pf):
    b, ns, _, qb = qt.shape
    s = k.shape[1]
    topk = min(TOPK_MAX, s // 4)
    batch_spec = lambda shape: pl.BlockSpec((1,) + shape[1:], lambda bi, i: (bi,) + (0,) * (len(shape) - 1),
                                            pipeline_mode=pl.Buffered(1))
    tile_spec = lambda r: pl.BlockSpec((1, 1, r, qb), lambda bi, i: (bi, i, 0, 0))
    return pl.pallas_call(
        functools.partial(_dsa_kernel, topk),
        grid=(b, ns),
        in_specs=[
            tile_spec(qt.shape[2]), tile_spec(qit.shape[2]), tile_spec(wit.shape[2]),
            batch_spec(k.shape), batch_spec(kidx.shape), batch_spec(vt.shape), _resident(pf.shape),
        ],
        out_specs=pl.BlockSpec((1, qb, D_ATTN), lambda bi, i: (bi, i, 0)),
        out_shape=jax.ShapeDtypeStruct((b, s, D_ATTN), MXU_DTYPE),
        scratch_shapes=[
            pltpu.VMEM((s + SEQ_TILE, qb), jnp.float32),
            pltpu.VMEM((N_HEADS * V_PAD, qb), jnp.float32),
            pltpu.VMEM((N_HEADS, 2 * LANES, qb), MXU_DTYPE),
            pltpu.VMEM((N_HEADS, 2 * SEQ_TILE, qb), jnp.float32),
        ],
        compiler_params=pltpu.CompilerParams(
            dimension_semantics=("arbitrary", "arbitrary"), vmem_limit_bytes=VMEM_LIMIT_BYTES),
        name="dsa",
    )(qt, qit, wit, k, kidx, vt, pf)


def _tail_kernel(alpha, attn_ref, conv_ref, x1_ref, p_ref, woa_ref, woc_ref, g2_ref, b2_ref,
                 wg_ref, wu_ref, wd_ref, g3_ref, b3_ref, wgate_ref, wproj_ref, o_ref, act_ref):
    f32 = jnp.float32
    mix = jnp.dot(attn_ref[...], woa_ref[...], preferred_element_type=f32) \
        + jnp.dot(conv_ref[...], woc_ref[...], preferred_element_type=f32)
    x2 = _layer_norm(alpha * x1_ref[...] + mix, g2_ref[...], b2_ref[...])
    ffn = _swiglu(x2.astype(MXU_DTYPE), wg_ref, wu_ref, wd_ref, act_ref)
    x3 =_layer_norm(alpha * x2 + 0.5 * ffn, g3_ref[...], b3_ref[...])
    gate = jax.nn.sigmoid(jnp.dot(x3.astype(MXU_DTYPE), wgate_ref[...], preferred_element_type=f32))
    proj = jnp.dot(p_ref[...].astype(MXU_DTYPE), wproj_ref[...], preferred_element_type=f32)
    o_ref[...] = x3 + gate * proj


def _tail(attn, conv, x1, p, woa, woc, g2, b2, wg3, wu3, wd3, g3, b3, wgate, wproj, alpha):
    n, d = x1.shape
    tm = min(ROW_TILE, n)
    row = lambda w: pl.BlockSpec((tm, w), lambda i: (i, 0))
    weights = (woa, woc, g2, b2, wg3, wu3, wd3, g3, b3, wgate, wproj)
    return pl.pallas_call(
        functools.partial(_tail_kernel, alpha),
        grid=(n // tm,),
        in_specs=[row(attn.shape[1]), row(conv.shape[1]), row(d), row(p.shape[1])]
        + [_resident(w.shape) for w in weights],
        out_specs=row(d),
        out_shape=jax.ShapeDtypeStruct((n, d), jnp.float32),
        scratch_shapes=[pltpu.VMEM((tm, wd3.shape[0]), MXU_DTYPE)],
        compiler_params=pltpu.CompilerParams(
            dimension_semantics=("arbitrary",), vmem_limit_bytes=VMEM_LIMIT_BYTES),
        name="tail",
    )(attn, conv, x1, p, *weights)


def _ffn_weights(wg, wu, wd):
    assert wg.shape[1] % FF_CHUNK == 0
    return wg.astype(MXU_DTYPE), wu.astype(MXU_DTYPE), wd.astype(MXU_DTYPE)


def _key_position_features(s):
    pos = jnp.arange(s, dtype=jnp.int32)[:, None]
    col = jnp.arange(LANES, dtype=jnp.int32)[None, :]
    feat = jnp.where(col < 3, pos // CHUNK, jnp.where(col < 6, pos % CHUNK, jnp.where(col < 9, 1, 0)))
    return feat.astype(MXU_DTYPE)


def kernel(x, p, ln1_g, ln1_b, ffn1_wg, ffn1_wu, ffn1_wd, w_in, conv_w, w_out, ln2_g, ln2_b,
           ffn2_wg, ffn2_wu, ffn2_wd, ln3_g, ln3_b, ple_gate_w, ple_proj_w):
    b, s, d = x.shape
    depth = p.shape[0]
    alpha = (2 * depth) ** 0.25
    assert d == D_MODEL and s % SEQ_TILE == 0 and (b * s) % min(ROW_TILE, b * s) == 0
    o_q, o_k, o_v, o_qi = 0, D_ATTN, 2 * D_ATTN, 3 * D_ATTN
    o_ki = o_qi + N_IDX_HEADS * IDX_DIM
    o_wi = o_ki + IDX_DIM
    o_conv = o_wi + N_IDX_HEADS
    row2 = lambda v: v.reshape(1, -1)

    for i in range(depth):
        wi = w_in[i]
        x1 = _ffn_ln(x.reshape(b * s, d), *_ffn_weights(ffn1_wg[i], ffn1_wu[i], ffn1_wd[i]),
                     row2(ln1_g[i]), row2(ln1_b[i]), alpha)
        wk = wi[:, o_k:o_v].astype(MXU_DTYPE)
        wki = jnp.pad(wi[:, o_ki:o_wi], ((0, 0), (0, LANES - IDX_DIM))).astype(MXU_DTYPE)
        wqt = (wi[:, o_q:o_k] * (LOG2E * HEAD_DIM ** -0.5)).T.astype(MXU_DTYPE)
        wvt = wi[:, o_v:o_qi].T.astype(MXU_DTYPE)
        wqit = wi[:, o_qi:o_ki].T.astype(MXU_DTYPE)
        wwit = jnp.pad(wi[:, o_wi:o_conv].T, ((0, LANES - N_IDX_HEADS), (0, 0))).astype(MXU_DTYPE)
        wc = wi[:, o_conv:].astype(MXU_DTYPE)
        k, kidx, qt, vt, qit, wit, conv = _in_proj(
            x1.reshape(b, s, d), wk, wki, wqt, wvt, wqit, wwit, wc, conv_w[i])
        attn = _dsa(qt, qit, wit, k, kidx, vt, _key_position_features(s))
        wo = w_out[i].astype(MXU_DTYPE)
        x = _tail(attn.reshape(b * s, D_ATTN), conv.reshape(b * s, D_CONV), x1, p[i].reshape(b * s, -1),
                  wo[:D_ATTN], wo[D_ATTN:], row2(ln2_g[i]), row2(ln2_b[i]),
                  *_ffn_weights(ffn2_wg[i], ffn2_wu[i], ffn2_wd[i]),
                  row2(ln3_g[i]), row2(ln3_b[i]),
                  ple_gate_w[i].astype(MXU_DTYPE), ple_proj_w[i].astype(MXU_DTYPE), alpha).reshape(b, s, d)
    return x
```

```python
import functools

import jax
import jax.numpy as jnp
from jax import lax
from jax.experimental import pallas as pl
from jax.experimental.pallas import tpu as pltpu

D_MODEL = 1024
CHUNK = 64
HEAD_DIM = 64
N_HEADS = 8
D_ATTN = N_HEADS * HEAD_DIM
N_IDX_HEADS = 8
IDX_DIM = 64
TOPK_MAX = 256
D_CONV = D_MODEL - D_ATTN
LN_EPS = 1e-5
IDX_SCALE = (N_IDX_HEADS ** -0.5) * (IDX_DIM ** -0.5)
LOG2E = 1.4426950408889634

MXU_DTYPE = jnp.bfloat16

LANES = 128
FF_CHUNK = 256
ROW_TILE = 512
SEQ_TILE = 256
HEAD_PAD = 2 * HEAD_DIM
V_PAD = HEAD_DIM + 16
POS_ROWS = 16
REDUCE_CHAINS = 4
BISECT_ROUNDS = 14
VMEM_LIMIT_BYTES = 56 * 1024 * 1024


def _layer_norm(y, g, b):
    mu = jnp.mean(y, axis=-1, keepdims=True)
    yc = y - mu
    var = jnp.mean(yc * yc, axis=-1, keepdims=True)
    return yc * lax.rsqrt(var + LN_EPS) * g + b


def _swiglu(xb, wg_ref, wu_ref, wd_ref, act_ref):
    for c in range(wg_ref.shape[1] // FF_CHUNK):
        cols = slice(c * FF_CHUNK, (c + 1) * FF_CHUNK)
        g = jnp.dot(xb, wg_ref[:, cols], preferred_element_type=jnp.float32)
        u = jnp.dot(xb, wu_ref[:, cols], preferred_element_type=jnp.float32)
        act_ref[:, cols] = ((g * jax.nn.sigmoid(g)) * u).astype(act_ref.dtype)
    return jnp.dot(act_ref[...], wd_ref[...], preferred_element_type=jnp.float32)


def _ffn_ln_kernel(alpha, x_ref, wg_ref, wu_ref, wd_ref, g_ref, b_ref, o_ref, act_ref):
    x = x_ref[...]
    ffn = _swiglu(x.astype(MXU_DTYPE), wg_ref, wu_ref, wd_ref, act_ref)
    o_ref[...] = _layer_norm(alpha * x + 0.5 * ffn, g_ref[...], b_ref[...])


def _resident(shape):
    nd = len(shape)
    return pl.BlockSpec(shape, lambda *_: (0,) * nd, pipeline_mode=pl.Buffered(1))


def _ffn_ln(x2d, wg3, wu3, wd3, g, b, alpha):
    n, d = x2d.shape
    tm = min(ROW_TILE, n)
    return pl.pallas_call(
        functools.partial(_ffn_ln_kernel, alpha),
        grid=(n // tm,),
        in_specs=[
            pl.BlockSpec((tm, d), lambda i: (i, 0)),
            _resident(wg3.shape), _resident(wu3.shape), _resident(wd3.shape),
            _resident(g.shape), _resident(b.shape),
        ],
        out_specs=pl.BlockSpec((tm, d), lambda i: (i, 0)),
        out_shape=jax.ShapeDtypeStruct((n, d), jnp.float32),
        scratch_shapes=[pltpu.VMEM((tm, wd3.shape[0]), MXU_DTYPE)],
        compiler_params=pltpu.CompilerParams(
            dimension_semantics=("arbitrary",), vmem_limit_bytes=VMEM_LIMIT_BYTES),
        name="ffn_ln",
    )(x2d, wg3, wu3, wd3, g, b)


def _in_proj_kernel(x_ref, wk_ref, wki_ref, wqt_ref, wvt_ref, wqit_ref, wwit_ref, wc_ref, cw_ref,
                    k_ref, kidx_ref, qt_ref, vt_ref, qit_ref, wit_ref, conv_ref, zbuf_ref):
    j = pl.program_id(1)
    tm = x_ref.shape[1]
    xb = x_ref[0].astype(MXU_DTYPE)
    nt = (((1,), (1,)), ((), ()))

    k_ref[0] = jnp.dot(xb, wk_ref[...], preferred_element_type=jnp.float32).astype(k_ref.dtype)
    kidx_ref[0] = jnp.dot(xb, wki_ref[...], preferred_element_type=jnp.float32).astype(kidx_ref.dtype)
    def padded_heads(yt, upper_half_of_head):
        zero = jnp.zeros((HEAD_DIM, tm), jnp.float32)
        groups = []
        for h in range(yt.shape[0] // HEAD_DIM):
            rows = yt[h * HEAD_DIM:(h + 1) * HEAD_DIM]
            groups += [zero, rows] if upper_half_of_head(h) else [rows, zero]
        return jnp.concatenate(groups, axis=0)

    qt = lax.dot_general(wqt_ref[...], xb, nt, preferred_element_type=jnp.float32)
    qt_ref[0, 0] = padded_heads(qt, lambda h: h % 2 == 1).astype(qt_ref.dtype)
    vt =lax.dot_general(wvt_ref[...], xb, nt, preferred_element_type=jnp.float32)
    ones = jnp.ones((V_PAD - HEAD_DIM, tm), jnp.float32)
    vt_ref[0, 0] = jnp.concatenate(
        [piece for h in range(N_HEADS) for piece in (vt[h * HEAD_DIM:(h + 1) * HEAD_DIM], ones)],
        axis=0).astype(vt_ref.dtype)
    qit = lax.dot_general(wqit_ref[...], xb, nt, preferred_element_type=jnp.float32)
    qit_ref[0, 0] = padded_heads(qit, lambda h: False).astype(qit_ref.dtype)
    wit = lax.dot_general(wwit_ref[...], xb, nt, preferred_element_type=jnp.float32)
    wit_ref[0, 0] = wit[0:N_IDX_HEADS, :] * IDX_SCALE

    h = jnp.dot(xb, wc_ref[...], preferred_element_type=jnp.float32)
    bg = h[:, 0:D_CONV]
    z = h[:, D_CONV:2 * D_CONV] * h[:, 2 * D_CONV:3 * D_CONV]

    @pl.when(j == 0)
    def _():
        zbuf_ref[0:8, :] = jnp.zeros((8, D_CONV), jnp.float32)

    zbuf_ref[8:8 + tm, :] = z
    z1 = zbuf_ref[7:7 + tm, :]
    z2 = zbuf_ref[6:6 + tm, :]
    y = (cw_ref[0:1, :] * z2 + cw_ref[1:2, :] * z1) + cw_ref[2:3, :] * z
    conv_ref[0] = (bg * y).astype(conv_ref.dtype)
    zbuf_ref[0:8, :] = z[tm - 8:tm, :]


def _in_proj(x1, wk, wki, wqt, wvt, wqit, wwit, wc, cw):
    b, s, d = x1.shape
    tm = SEQ_TILE
    ns = s // tm
    f32 = jnp.float32
    out_shape = (
        jax.ShapeDtypeStruct((b, s, D_ATTN), MXU_DTYPE),
        jax.ShapeDtypeStruct((b, s, LANES), MXU_DTYPE),
        jax.ShapeDtypeStruct((b, ns, N_HEADS * HEAD_PAD, tm), MXU_DTYPE),
        jax.ShapeDtypeStruct((b, ns, N_HEADS * V_PAD, tm), MXU_DTYPE),
        jax.ShapeDtypeStruct((b, ns, N_IDX_HEADS * HEAD_PAD, tm), MXU_DTYPE),
        jax.ShapeDtypeStruct((b, ns, N_IDX_HEADS, tm), f32),
        jax.ShapeDtypeStruct((b, s, D_CONV), MXU_DTYPE),
    )
    row_spec = lambda w: pl.BlockSpec((1, tm, w), lambda bi, j: (bi, j, 0))
    tile_spec = lambda r: pl.BlockSpec((1, 1, r, tm), lambda bi, j: (bi, j, 0, 0))
    return pl.pallas_call(
        _in_proj_kernel,
        grid=(b, ns),
        in_specs=[
            pl.BlockSpec((1, tm, d), lambda bi, j: (bi, j, 0)),
            _resident(wk.shape), _resident(wki.shape), _resident(wqt.shape), _resident(wvt.shape),
            _resident(wqit.shape), _resident(wwit.shape), _resident(wc.shape), _resident(cw.shape),
        ],
        out_specs=(
            row_spec(D_ATTN), row_spec(LANES), tile_spec(N_HEADS * HEAD_PAD), tile_spec(N_HEADS * V_PAD),
            tile_spec(N_IDX_HEADS * HEAD_PAD), tile_spec(N_IDX_HEADS), row_spec(D_CONV),
        ),
        out_shape=out_shape,
        scratch_shapes=[pltpu.VMEM((tm + 8, D_CONV), f32)],
        compiler_params=pltpu.CompilerParams(
            dimension_semantics=("arbitrary", "arbitrary"), vmem_limit_bytes=VMEM_LIMIT_BYTES),
        name="in_proj",
    )(x1, wk, wki, wqt, wvt, wqit, wwit, wc, cw)


def _col_reduce(op, x):
    r, c = x.shape
    chains = [None] * REDUCE_CHAINS
    for g in range(r // 8):
        part = x[g * 8:(g + 1) * 8, :]
        k = g % REDUCE_CHAINS
        chains[k] = part if chains[k] is None else op(chains[k], part)
    out = chains[0]
    for extra in chains[1:]:
        if extra is not None:
            out = op(out, extra)
    return out


def _col_count(mask):
    return _col_reduce(jnp.add, jnp.where(mask, 1, 0).astype(jnp.int32))


def _col_min(x):
    return _col_reduce(jnp.minimum, x)


def _dsa_kernel(topk, qt_ref, qit_ref, wit_ref, k_ref, kidx_ref, vt_ref, pf_ref, o_ref,
                s_ref, acc_ref, rhs_ref, lg_ref):
    i = pl.program_id(1)
    kt = SEQ_TILE
    qb = qt_ref.shape[3]
    f32 = jnp.float32
    inf = jnp.float32(jnp.inf)
    n_tiles = i + 1

    chunk_shift = CHUNK.bit_length() - 1
    lane_pos = lax.broadcasted_iota(jnp.int32, (1, qb), 1)
    row_pos = lax.broadcasted_iota(jnp.int32, (kt, 1), 0)
    q_pos = i * qb + lane_pos
    n_adm = ((q_pos >> chunk_shift) + 1) * CHUNK

    def tile_rows(t):
        return pl.ds(pl.multiple_of(t * kt, kt), kt)

    def score_rows(rows, n_rows):
        kx = kidx_ref[0, rows, :]
        s = jnp.zeros((n_rows, qb), f32)
        for h in range(N_IDX_HEADS):
            rel = jnp.dot(kx, qit_ref[0, 0, h * HEAD_PAD:(h + 1) * HEAD_PAD, :],
                          preferred_element_type=f32)
            s = s + jnp.maximum(rel, 0.0) * wit_ref[0, 0, h:h + 1, :]
        return s

    def score_tile(t):
        return score_rows(tile_rows(t), kt)

    def phase_a(j, carry):
        smax, smin = carry
        rows = pl.ds(pl.multiple_of(j * (2 * kt), 2 * kt), 2 * kt)
        s = score_rows(rows, 2 * kt)
        s_ref[rows, :] = s
        second_counted = 2 * j + 1 < i
        for half, counted in ((s[0:kt], None), (s[kt:2 * kt], second_counted)):
            tmax = jnp.max(half, axis=0, keepdims=True)
            tmin = jnp.min(half, axis=0, keepdims=True)
            if counted is not None:
                tmax = jnp.where(counted, tmax, -inf)
                tmin = jnp.where(counted, tmin, inf)
            smax = jnp.maximum(smax, tmax)
            smin = jnp.minimum(smin, tmin)
        return smax, smin

    smax, smin = lax.fori_loop(0, (i + 1) >> 1, phase_a,
                               (jnp.full((1, qb), -inf), jnp.full((1, qb), inf)))
    s = score_tile(i)
    adm = (row_pos >> chunk_shift) <= (lane_pos >> chunk_shift)
    s_ref[tile_rows(i), :] = jnp.where(adm, s, -inf)
    smax = jnp.maximum(smax, jnp.max(jnp.where(adm, s, -inf), axis=0, keepdims=True))
    smin = jnp.minimum(smin, jnp.min(jnp.where(adm, s, inf), axis=0, keepdims=True))

    s_ref[tile_rows(n_tiles), :] = jnp.full((kt, qb), -inf)

    def sweep(fn, init):
        def body(j, carry):
            start = pl.multiple_of(j * (2 * kt), 2 * kt)
            return fn(s_ref[pl.ds(start, 2 * kt), :], start, carry)
        return lax.fori_loop(0, (n_tiles + 1) >> 1, body, init)

    def count_ge(p):
        c8 = sweep(lambda tile, start, c: c + _col_count(tile >= p), jnp.zeros((8, qb), jnp.int32))
        return jnp.sum(c8, axis=0, keepdims=True)

    searching = n_adm > topk

    def bisect(_, carry):
        lo, hi, c_lo = carry
        p = lo * 0.5 + hi * 0.5
        c = count_ge(p)
        up = c >= topk
        return jnp.where(up, p, lo), jnp.where(up, hi, p), jnp.where(up, c, c_lo)

    lo, _, c_lo = lax.fori_loop(0, BISECT_ROUNDS, bisect, (smin, smax, n_adm))
    thr = jnp.min(sweep(lambda tile, start, c: jnp.minimum(c, _col_min(jnp.where(tile >= lo, tile, inf))),
                        jnp.full((8, qb), inf)), axis=0, keepdims=True)

    def above(thr):
        def fn(tile, start, carry):
            gt8, nx8 = carry
            gt = tile > thr
            return gt8 + _col_count(gt), jnp.minimum(nx8, _col_min(jnp.where(gt, tile, inf)))
        gt8, nx8 = sweep(fn, (jnp.zeros((8, qb), jnp.int32), jnp.full((8, qb), inf)))
        return jnp.sum(gt8, axis=0, keepdims=True), jnp.min(nx8, axis=0, keepdims=True)

    def settle(thr):
        def peel_cond(carry):
            _, _, n_gt, _ = carry
            return jnp.max(n_gt) >= topk

        def peel_body(carry):
            thr, n_ge, n_gt, nxt = carry
            step = n_gt >= topk
            thr = jnp.where(step, nxt, thr)
            return (thr, jnp.where(step, n_gt, n_ge)) + above(thr)

        thr, n_ge, n_gt, _ = lax.while_loop(peel_cond, peel_body, (thr, c_lo) + above(thr))
        n_ties_kept = topk - n_gt
        cut_needed = jnp.logical_and(searching, n_ge > topk)

        @pl.when(jnp.max(cut_needed.astype(jnp.int32)) > 0)
        def _():
            r_i = lax.broadcasted_iota(jnp.int32, (2 * kt, 2 * kt), 0)
            c_i = lax.broadcasted_iota(jnp.int32, (2 * kt, 2 * kt), 1)
            prefix_op = jnp.where(c_i <= r_i, 1.0, 0.0).astype(MXU_DTYPE)
            keep = jnp.where(cut_needed, n_ties_kept.astype(f32), inf)

            def knock(tile, start, seen):
                tie = tile == thr
                rank = jnp.dot(prefix_op, jnp.where(tie, 1.0, 0.0).astype(MXU_DTYPE),
                               preferred_element_type=f32) + seen
                s_ref[pl.ds(start, 2 * kt), :] = jnp.where(jnp.logical_and(tie, rank > keep), -inf, tile)
                return rank[2 * kt - 1:2 * kt, :]
            sweep(knock, jnp.zeros((1, qb), f32))

        return thr

    unsettled = jnp.logical_and(searching, c_lo != topk)
    thr = lax.cond(jnp.max(unsettled.astype(jnp.int32)) > 0, settle, lambda t: t, thr)
    thr = jnp.where(searching, thr, smin)


    acc_ref[...] = jnp.zeros(acc_ref.shape, f32)
    slopes = [LOG2E * 2.0 ** (-8.0 * (h + 1) / N_HEADS) for h in range(N_HEADS)]
    q_posf = q_pos.astype(f32)

    def three_pieces(x):
        hi = x.astype(MXU_DTYPE).astype(f32)
        mid = (x - hi).astype(MXU_DTYPE).astype(f32)
        return [hi, mid, x - hi - mid]

    for h in range(N_HEADS):
        pieces = (three_pieces(jnp.full((1, qb), slopes[h] * CHUNK, f32))
                  + three_pieces(jnp.full((1, qb), slopes[h], f32))
                  + three_pieces(-slopes[h] * q_posf))
        pos_rows = jnp.concatenate(pieces + [jnp.zeros((POS_ROWS - len(pieces), qb), f32)], axis=0)
        rhs_ref[h, 0:HEAD_PAD, :] = qt_ref[0, 0, h * HEAD_PAD:(h + 1) * HEAD_PAD, :]
        rhs_ref[h, HEAD_PAD:HEAD_PAD + POS_ROWS, :] = pos_rows.astype(rhs_ref.dtype)
        rhs_ref[h, HEAD_PAD + POS_ROWS:, :] = jnp.zeros((LANES - POS_ROWS, qb), rhs_ref.dtype)

    def logits_head(t, slot, thr_t, diagonal=False):
        rows = tile_rows(t)
        mask_bias = jnp.where(s_ref[rows, :] >= thr_t, 0.0, -inf)
        pf = pf_ref[rows, :]
        if diagonal:
            twice_future = 2.0 * jnp.maximum(row_pos - lane_pos, 0).astype(f32)

        def head(h):
            lhs = jnp.concatenate([k_ref[0, rows, (h // 2) * LANES:(h // 2 + 1) * LANES], pf], axis=1)
            logit = jnp.dot(lhs, rhs_ref[h], preferred_element_type=f32) + mask_bias
            if diagonal:
                logit = logit - slopes[h] * twice_future
            lg_ref[h, slot * kt:(slot + 1) * kt, :] = logit
            return jnp.max(logit, axis=0, keepdims=True)
        return head

    def softmax_head(t, slot, m_old, tile_max, h):
        m_new = jnp.maximum(m_old, tile_max)
        m_safe = jnp.where(m_new == -inf, 0.0, m_new)
        alpha = jnp.exp2(m_old - m_safe)
        p = jnp.exp2(lg_ref[h, slot * kt:(slot + 1) * kt, :] - m_safe).astype(MXU_DTYPE)
        hrows = slice(h * V_PAD, (h + 1) * V_PAD)
        pv = jnp.dot(vt_ref[0, t, hrows, :], p, preferred_element_type=f32)
        acc_ref[hrows, :] = alpha * acc_ref[hrows, :] + pv
        return m_new

    def rows_of(x):
        return [x[h:h + 1, :] for h in range(N_HEADS)]

    def overlapped(t_soft, slot_soft, m_all, tile_max, t_next, slot_next, thr_next):
        head = logits_head(t_next, slot_next, thr_next)
        m_rows, max_rows = [], []
        for h, (m_h, max_h) in enumerate(zip(rows_of(m_all), rows_of(tile_max))):
            max_rows.append(head(h))
            m_rows.append(softmax_head(t_soft, slot_soft, m_h, max_h, h))
        return jnp.concatenate(m_rows, axis=0), jnp.concatenate(max_rows, axis=0)

    def tile_pair(j, carry):
        m_all, max_a = carry
        t0 = 2 * j
        t1 = t0 + 1
        m_all, max_b = overlapped(t0, 0, m_all, max_a, t1, 1, jnp.where(t1 < i, thr, inf))
        m_all, max_a = overlapped(t1, 1, m_all, max_b, jnp.minimum(t0 + 2, i), 0, thr)
        return m_all, max_a

    first = logits_head(0, 0, thr)
    m_all, _ = lax.fori_loop(
        0, (i + 1) >> 1, tile_pair,
        (jnp.full((N_HEADS, qb), -inf), jnp.concatenate([first(h) for h in range(N_HEADS)], axis=0)))
    last = logits_head(i, 0, thr, diagonal=True)
    last_max = [last(h) for h in range(N_HEADS)]
    for h, m_h in enumerate(rows_of(m_all)):
        softmax_head(i, 0, m_h, last_max[h], h)


    out = [acc_ref[h * V_PAD:h * V_PAD + HEAD_DIM, :] / acc_ref[h * V_PAD + HEAD_DIM:h * V_PAD + HEAD_DIM + 1, :]
           for h in range(N_HEADS)]
    o_ref[0] = jnp.concatenate(out, axis=0).T.astype(o_ref.dtype)


def _dsa(qt, qit, wit, k, kidx, vt, pf):
    b, ns, _, qb = qt.shape
    s = k.shape[1]
    topk = min(TOPK_MAX, s // 4)
    batch_spec = lambda shape: pl.BlockSpec((1,) + shape[1:], lambda bi, i: (bi,) + (0,) * (len(shape) - 1),
                                            pipeline_mode=pl.Buffered(1))
    tile_spec = lambda r: pl.BlockSpec((1, 1, r, qb), lambda bi, i: (bi, i, 0, 0))
    return pl.pallas_call(
        functools.partial(_dsa_kernel, topk),
        grid=(b, ns),
        in_specs=[
            tile_spec(qt.shape[2]), tile_spec(qit.shape[2]), tile_spec(wit.shape[2]),
            batch_spec(k.shape), batch_spec(kidx.shape), batch_spec(vt.shape), _resident(pf.shape),
        ],
        out_specs=pl.BlockSpec((1, qb, D_ATTN), lambda bi, i: (bi, i, 0)),
        out_shape=jax.ShapeDtypeStruct((b, s, D_ATTN), MXU_DTYPE),
        scratch_shapes=[
            pltpu.VMEM((s + SEQ_TILE, qb), jnp.float32),
            pltpu.VMEM((N_HEADS * V_PAD, qb), jnp.float32),
            pltpu.VMEM((N_HEADS, 2 * LANES, qb), MXU_DTYPE),
            pltpu.VMEM((N_HEADS, 2 * SEQ_TILE, qb), jnp.float32),
        ],
        compiler_params=pltpu.CompilerParams(
            dimension_semantics=("arbitrary", "arbitrary"), vmem_limit_bytes=VMEM_LIMIT_BYTES),
        name="dsa",
    )(qt, qit, wit, k, kidx, vt, pf)


def _tail_kernel(alpha, attn_ref, conv_ref, x1_ref, p_ref, woa_ref, woc_ref, g2_ref, b2_ref,
                 wg_ref, wu_ref, wd_ref, g3_ref, b3_ref, wgate_ref, wproj_ref, o_ref, act_ref):
    f32 = jnp.float32
    mix = jnp.dot(attn_ref[...], woa_ref[...], preferred_element_type=f32) \
        + jnp.dot(conv_ref[...], woc_ref[...], preferred_element_type=f32)
    x2 = _layer_norm(alpha * x1_ref[...] + mix, g2_ref[...], b2_ref[...])
    ffn = _swiglu(x2.astype(MXU_DTYPE), wg_ref, wu_ref, wd_ref, act_ref)
    x3 =_layer_norm(alpha * x2 + 0.5 * ffn, g3_ref[...], b3_ref[...])
    gate = jax.nn.sigmoid(jnp.dot(x3.astype(MXU_DTYPE), wgate_ref[...], preferred_element_type=f32))
    proj = jnp.dot(p_ref[...].astype(MXU_DTYPE), wproj_ref[...], preferred_element_type=f32)
    o_ref[...] = x3 + gate * proj


def _tail(attn, conv, x1, p, woa, woc, g2, b2, wg3, wu3, wd3, g3, b3, wgate, wproj, alpha):
    n, d = x1.shape
    tm = min(ROW_TILE, n)
    row = lambda w: pl.BlockSpec((tm, w), lambda i: (i, 0))
    weights = (woa, woc, g2, b2, wg3, wu3, wd3, g3, b3, wgate, wproj)
    return pl.pallas_call(
        functools.partial(_tail_kernel, alpha),
        grid=(n // tm,),
        in_specs=[row(attn.shape[1]), row(conv.shape[1]), row(d), row(p.shape[1])]
        + [_resident(w.shape) for w in weights],
        out_specs=row(d),
        out_shape=jax.ShapeDtypeStruct((n, d), jnp.float32),
        scratch_shapes=[pltpu.VMEM((tm, wd3.shape[0]), MXU_DTYPE)],
        compiler_params=pltpu.CompilerParams(
            dimension_semantics=("arbitrary",), vmem_limit_bytes=VMEM_LIMIT_BYTES),
        name="tail",
    )(attn, conv, x1, p, *weights)


def _ffn_weights(wg, wu, wd):
    assert wg.shape[1] % FF_CHUNK == 0
    return wg.astype(MXU_DTYPE), wu.astype(MXU_DTYPE), wd.astype(MXU_DTYPE)


def _key_position_features(s):
    pos = jnp.arange(s, dtype=jnp.int32)[:, None]
    col = jnp.arange(LANES, dtype=jnp.int32)[None, :]
    feat = jnp.where(col < 3, pos // CHUNK, jnp.where(col < 6, pos % CHUNK, jnp.where(col < 9, 1, 0)))
    return feat.astype(MXU_DTYPE)


def kernel(x, p, ln1_g, ln1_b, ffn1_wg, ffn1_wu, ffn1_wd, w_in, conv_w, w_out, ln2_g, ln2_b,
           ffn2_wg, ffn2_wu, ffn2_wd, ln3_g, ln3_b, ple_gate_w, ple_proj_w):
    b, s, d = x.shape
    depth = p.shape[0]
    alpha = (2 * depth) ** 0.25
    assert d == D_MODEL and s % SEQ_TILE == 0 and (b * s) % min(ROW_TILE, b * s) == 0
    o_q, o_k, o_v, o_qi = 0, D_ATTN, 2 * D_ATTN, 3 * D_ATTN
    o_ki = o_qi + N_IDX_HEADS * IDX_DIM
    o_wi = o_ki + IDX_DIM
    o_conv = o_wi + N_IDX_HEADS
    row2 = lambda v: v.reshape(1, -1)

    for i in range(depth):
        wi = w_in[i]
        x1 = _ffn_ln(x.reshape(b * s, d), *_ffn_weights(ffn1_wg[i], ffn1_wu[i], ffn1_wd[i]),
                     row2(ln1_g[i]), row2(ln1_b[i]), alpha)
        wk = wi[:, o_k:o_v].astype(MXU_DTYPE)
        wki = jnp.pad(wi[:, o_ki:o_wi], ((0, 0), (0, LANES - IDX_DIM))).astype(MXU_DTYPE)
        wqt = (wi[:, o_q:o_k] * (LOG2E * HEAD_DIM ** -0.5)).T.astype(MXU_DTYPE)
        wvt = wi[:, o_v:o_qi].T.astype(MXU_DTYPE)
        wqit = wi[:, o_qi:o_ki].T.astype(MXU_DTYPE)
        wwit = jnp.pad(wi[:, o_wi:o_conv].T, ((0, LANES - N_IDX_HEADS), (0, 0))).astype(MXU_DTYPE)
        wc = wi[:, o_conv:].astype(MXU_DTYPE)
        k, kidx, qt, vt, qit, wit, conv = _in_proj(
            x1.reshape(b, s, d), wk, wki, wqt, wvt, wqit, wwit, wc, conv_w[i])
        attn = _dsa(qt, qit, wit, k, kidx, vt, _key_position_features(s))
        wo = w_out[i].astype(MXU_DTYPE)
        x = _tail(attn.reshape(b * s, D_ATTN), conv.reshape(b * s, D_CONV), x1, p[i].reshape(b * s, -1),
                  wo[:D_ATTN], wo[D_ATTN:], row2(ln2_g[i]), row2(ln2_b[i]),
                  *_ffn_weights(ffn2_wg[i], ffn2_wu[i], ffn2_wd[i]),
                  row2(ln3_g[i]), row2(ln3_b[i]),
                  ple_gate_w[i].astype(MXU_DTYPE), ple_proj_w[i].astype(MXU_DTYPE), alpha).reshape(b, s, d)
    return x
```

```python
import functools

import jax
import jax.numpy as jnp
from jax import lax
from jax.experimental import pallas as pl
from jax.experimental.pallas import tpu as pltpu

D_MODEL = 1024
CHUNK = 64
HEAD_DIM = 64
N_HEADS = 8
D_ATTN = N_HEADS * HEAD_DIM
N_IDX_HEADS = 8
IDX_DIM = 64
TOPK_MAX = 256
D_CONV = D_MODEL - D_ATTN
LN_EPS = 1e-5
IDX_SCALE = (N_IDX_HEADS ** -0.5) * (IDX_DIM ** -0.5)
LOG2E = 1.4426950408889634

MXU_DTYPE = jnp.bfloat16

LANES = 128
FF_CHUNK = 256
ROW_TILE = 512
SEQ_TILE = 256
HEAD_PAD = 2 * HEAD_DIM
V_PAD = HEAD_DIM + 16
POS_ROWS = 16
LOGITS_AHEAD = 2
REDUCE_CHAINS = 4
BISECT_ROUNDS = 14
VMEM_LIMIT_BYTES = 56 * 1024 * 1024


def _layer_norm(y, g, b):
    mu = jnp.mean(y, axis=-1, keepdims=True)
    yc = y - mu
    var = jnp.mean(yc * yc, axis=-1, keepdims=True)
    return yc * lax.rsqrt(var + LN_EPS) * g + b


def _swiglu(xb, wg_ref, wu_ref, wd_ref, act_ref):
    for c in range(wg_ref.shape[1] // FF_CHUNK):
        cols = slice(c * FF_CHUNK, (c + 1) * FF_CHUNK)
        g = jnp.dot(xb, wg_ref[:, cols], preferred_element_type=jnp.float32)
        u = jnp.dot(xb, wu_ref[:, cols], preferred_element_type=jnp.float32)
        act_ref[:, cols] = ((g * jax.nn.sigmoid(g)) * u).astype(act_ref.dtype)
    return jnp.dot(act_ref[...], wd_ref[...], preferred_element_type=jnp.float32)


def _ffn_ln_kernel(alpha, x_ref, wg_ref, wu_ref, wd_ref, g_ref, b_ref, o_ref, act_ref):
    x = x_ref[...]
    ffn = _swiglu(x.astype(MXU_DTYPE), wg_ref, wu_ref, wd_ref, act_ref)
    o_ref[...] = _layer_norm(alpha * x + 0.5 * ffn, g_ref[...], b_ref[...])


def _resident(shape):
    nd = len(shape)
    return pl.BlockSpec(shape, lambda *_: (0,) * nd, pipeline_mode=pl.Buffered(1))


def _ffn_ln(x2d, wg3, wu3, wd3, g, b, alpha):
    n, d = x2d.shape
    tm = min(ROW_TILE, n)
    return pl.pallas_call(
        functools.partial(_ffn_ln_kernel, alpha),
        grid=(n // tm,),
        in_specs=[
            pl.BlockSpec((tm, d), lambda i: (i, 0)),
            _resident(wg3.shape), _resident(wu3.shape), _resident(wd3.shape),
            _resident(g.shape), _resident(b.shape),
        ],
        out_specs=pl.BlockSpec((tm, d), lambda i: (i, 0)),
        out_shape=jax.ShapeDtypeStruct((n, d), jnp.float32),
        scratch_shapes=[pltpu.VMEM((tm, wd3.shape[0]), MXU_DTYPE)],
        compiler_params=pltpu.CompilerParams(
            dimension_semantics=("arbitrary",), vmem_limit_bytes=VMEM_LIMIT_BYTES),
        name="ffn_ln",
    )(x2d, wg3, wu3, wd3, g, b)


def _in_proj_kernel(x_ref, wk_ref, wki_ref, wqt_ref, wvt_ref, wqit_ref, wwit_ref, wc_ref, cw_ref,
                    k_ref, kidx_ref, qt_ref, vt_ref, qit_ref, wit_ref, conv_ref, zbuf_ref):
    j = pl.program_id(1)
    tm = x_ref.shape[1]
    xb = x_ref[0].astype(MXU_DTYPE)
    nt = (((1,), (1,)), ((), ()))

    k_ref[0] = jnp.dot(xb, wk_ref[...], preferred_element_type=jnp.float32).astype(k_ref.dtype)
    kidx_ref[0] = jnp.dot(xb, wki_ref[...], preferred_element_type=jnp.float32).astype(kidx_ref.dtype)
    def padded_heads(yt, upper_half_of_head):
        zero = jnp.zeros((HEAD_DIM, tm), jnp.float32)
        groups = []
        for h in range(yt.shape[0] // HEAD_DIM):
            rows = yt[h * HEAD_DIM:(h + 1) * HEAD_DIM]
            groups += [zero, rows] if upper_half_of_head(h) else [rows, zero]
        return jnp.concatenate(groups, axis=0)

    qt = lax.dot_general(wqt_ref[...], xb, nt, preferred_element_type=jnp.float32)
    qt_ref[0, 0] = padded_heads(qt, lambda h: h % 2 == 1).astype(qt_ref.dtype)
    vt =lax.dot_general(wvt_ref[...], xb, nt, preferred_element_type=jnp.float32)
    ones = jnp.ones((V_PAD - HEAD_DIM, tm), jnp.float32)
    vt_ref[0, 0] = jnp.concatenate(
        [piece for h in range(N_HEADS) for piece in (vt[h * HEAD_DIM:(h + 1) * HEAD_DIM], ones)],
        axis=0).astype(vt_ref.dtype)
    qit = lax.dot_general(wqit_ref[...], xb, nt, preferred_element_type=jnp.float32)
    qit_ref[0, 0] = padded_heads(qit, lambda h: False).astype(qit_ref.dtype)
    wit = lax.dot_general(wwit_ref[...], xb, nt, preferred_element_type=jnp.float32)
    wit_ref[0, 0] = wit[0:N_IDX_HEADS, :] * IDX_SCALE

    h = jnp.dot(xb, wc_ref[...], preferred_element_type=jnp.float32)
    bg = h[:, 0:D_CONV]
    z = h[:, D_CONV:2 * D_CONV] * h[:, 2 * D_CONV:3 * D_CONV]

    @pl.when(j == 0)
    def _():
        zbuf_ref[0:8, :] = jnp.zeros((8, D_CONV), jnp.float32)

    zbuf_ref[8:8 + tm, :] = z
    z1 = zbuf_ref[7:7 + tm, :]
    z2 = zbuf_ref[6:6 + tm, :]
    y = (cw_ref[0:1, :] * z2 + cw_ref[1:2, :] * z1) + cw_ref[2:3, :] * z
    conv_ref[0] = (bg * y).astype(conv_ref.dtype)
    zbuf_ref[0:8, :] = z[tm - 8:tm, :]


def _in_proj(x1, wk, wki, wqt, wvt, wqit, wwit, wc, cw):
    b, s, d = x1.shape
    tm = SEQ_TILE
    ns = s // tm
    f32 = jnp.float32
    out_shape = (
        jax.ShapeDtypeStruct((b, s, D_ATTN), MXU_DTYPE),
        jax.ShapeDtypeStruct((b, s, LANES), MXU_DTYPE),
        jax.ShapeDtypeStruct((b, ns, N_HEADS * HEAD_PAD, tm), MXU_DTYPE),
        jax.ShapeDtypeStruct((b, ns, N_HEADS * V_PAD, tm), MXU_DTYPE),
        jax.ShapeDtypeStruct((b, ns, N_IDX_HEADS * HEAD_PAD, tm), MXU_DTYPE),
        jax.ShapeDtypeStruct((b, ns, N_IDX_HEADS, tm), f32),
        jax.ShapeDtypeStruct((b, s, D_CONV), MXU_DTYPE),
    )
    row_spec = lambda w: pl.BlockSpec((1, tm, w), lambda bi, j: (bi, j, 0))
    tile_spec = lambda r: pl.BlockSpec((1, 1, r, tm), lambda bi, j: (bi, j, 0, 0))
    return pl.pallas_call(
        _in_proj_kernel,
        grid=(b, ns),
        in_specs=[
            pl.BlockSpec((1, tm, d), lambda bi, j: (bi, j, 0)),
            _resident(wk.shape), _resident(wki.shape), _resident(wqt.shape), _resident(wvt.shape),
            _resident(wqit.shape), _resident(wwit.shape), _resident(wc.shape), _resident(cw.shape),
        ],
        out_specs=(
            row_spec(D_ATTN), row_spec(LANES), tile_spec(N_HEADS * HEAD_PAD), tile_spec(N_HEADS * V_PAD),
            tile_spec(N_IDX_HEADS * HEAD_PAD), tile_spec(N_IDX_HEADS), row_spec(D_CONV),
        ),
        out_shape=out_shape,
        scratch_shapes=[pltpu.VMEM((tm + 8, D_CONV), f32)],
        compiler_params=pltpu.CompilerParams(
            dimension_semantics=("arbitrary", "arbitrary"), vmem_limit_bytes=VMEM_LIMIT_BYTES),
        name="in_proj",
    )(x1, wk, wki, wqt, wvt, wqit, wwit, wc, cw)


def _col_reduce(op, x):
    r, c = x.shape
    chains = [None] * REDUCE_CHAINS
    for g in range(r // 8):
        part = x[g * 8:(g + 1) * 8, :]
        k = g % REDUCE_CHAINS
        chains[k] = part if chains[k] is None else op(chains[k], part)
    out = chains[0]
    for extra in chains[1:]:
        if extra is not None:
            out = op(out, extra)
    return out


def _col_count(mask):
    return _col_reduce(jnp.add, jnp.where(mask, 1, 0).astype(jnp.int32))


def _col_min(x):
    return _col_reduce(jnp.minimum, x)


def _dsa_kernel(topk, qt_ref, qit_ref, wit_ref, k_ref, kidx_ref, vt_ref, pf_ref, o_ref,
                s_ref, acc_ref, rhs_ref, lg_ref):
    i = pl.program_id(1)
    kt = SEQ_TILE
    qb = qt_ref.shape[3]
    f32 = jnp.float32
    inf = jnp.float32(jnp.inf)
    n_tiles = i + 1

    chunk_shift = CHUNK.bit_length() - 1
    lane_pos = lax.broadcasted_iota(jnp.int32, (1, qb), 1)
    row_pos = lax.broadcasted_iota(jnp.int32, (kt, 1), 0)
    q_pos = i * qb + lane_pos
    n_adm = ((q_pos >> chunk_shift) + 1) * CHUNK

    def tile_rows(t):
        return pl.ds(pl.multiple_of(t * kt, kt), kt)

    def score_rows(rows, n_rows):
        kx = kidx_ref[0, rows, :]
        s = jnp.zeros((n_rows, qb), f32)
        for h in range(N_IDX_HEADS):
            rel = jnp.dot(kx, qit_ref[0, 0, h * HEAD_PAD:(h + 1) * HEAD_PAD, :],
                          preferred_element_type=f32)
            s = s + jnp.maximum(rel, 0.0) * wit_ref[0, 0, h:h + 1, :]
        return s

    def score_tile(t):
        return score_rows(tile_rows(t), kt)

    def phase_a(j, carry):
        smax, smin = carry
        rows = pl.ds(pl.multiple_of(j * (2 * kt), 2 * kt), 2 * kt)
        s = score_rows(rows, 2 * kt)
        s_ref[rows, :] = s
        second_counted = 2 * j + 1 < i
        for half, counted in ((s[0:kt], None), (s[kt:2 * kt], second_counted)):
            tmax = jnp.max(half, axis=0, keepdims=True)
            tmin = jnp.min(half, axis=0, keepdims=True)
            if counted is not None:
                tmax = jnp.where(counted, tmax, -inf)
                tmin = jnp.where(counted, tmin, inf)
            smax = jnp.maximum(smax, tmax)
            smin = jnp.minimum(smin, tmin)
        return smax, smin

    smax, smin = lax.fori_loop(0, (i + 1) >> 1, phase_a,
                               (jnp.full((1, qb), -inf), jnp.full((1, qb), inf)))
    s = score_tile(i)
    adm = (row_pos >> chunk_shift) <= (lane_pos >> chunk_shift)
    s_ref[tile_rows(i), :] = jnp.where(adm, s, -inf)
    smax = jnp.maximum(smax, jnp.max(jnp.where(adm, s, -inf), axis=0, keepdims=True))
    smin = jnp.minimum(smin, jnp.min(jnp.where(adm, s, inf), axis=0, keepdims=True))

    s_ref[tile_rows(n_tiles), :] = jnp.full((kt, qb), -inf)

    def sweep(fn, init):
        def body(j, carry):
            start = pl.multiple_of(j * (2 * kt), 2 * kt)
            return fn(s_ref[pl.ds(start, 2 * kt), :], start, carry)
        return lax.fori_loop(0, (n_tiles + 1) >> 1, body, init)

    def count_ge(p):
        c8 = sweep(lambda tile, start, c: c + _col_count(tile >= p), jnp.zeros((8, qb), jnp.int32))
        return jnp.sum(c8, axis=0, keepdims=True)

    searching = n_adm > topk

    def bisect(_, carry):
        lo, hi, c_lo = carry
        p = lo * 0.5 + hi * 0.5
        c = count_ge(p)
        up = c >= topk
        return jnp.where(up, p, lo), jnp.where(up, hi, p), jnp.where(up, c, c_lo)

    lo, _, c_lo = lax.fori_loop(0, BISECT_ROUNDS, bisect, (smin, smax, n_adm))
    thr = jnp.min(sweep(lambda tile, start, c: jnp.minimum(c, _col_min(jnp.where(tile >= lo, tile, inf))),
                        jnp.full((8, qb), inf)), axis=0, keepdims=True)

    def above(thr):
        def fn(tile, start, carry):
            gt8, nx8 = carry
            gt = tile > thr
            return gt8 + _col_count(gt), jnp.minimum(nx8, _col_min(jnp.where(gt, tile, inf)))
        gt8, nx8 = sweep(fn, (jnp.zeros((8, qb), jnp.int32), jnp.full((8, qb), inf)))
        return jnp.sum(gt8, axis=0, keepdims=True), jnp.min(nx8, axis=0, keepdims=True)

    def settle(thr):
        def peel_cond(carry):
            _, _, n_gt, _ = carry
            return jnp.max(n_gt) >= topk

        def peel_body(carry):
            thr, n_ge, n_gt, nxt = carry
            step = n_gt >= topk
            thr = jnp.where(step, nxt, thr)
            return (thr, jnp.where(step, n_gt, n_ge)) + above(thr)

        thr, n_ge, n_gt, _ = lax.while_loop(peel_cond, peel_body, (thr, c_lo) + above(thr))
        n_ties_kept = topk - n_gt
        cut_needed = jnp.logical_and(searching, n_ge > topk)

        @pl.when(jnp.max(cut_needed.astype(jnp.int32)) > 0)
        def _():
            r_i = lax.broadcasted_iota(jnp.int32, (2 * kt, 2 * kt), 0)
            c_i = lax.broadcasted_iota(jnp.int32, (2 * kt, 2 * kt), 1)
            prefix_op = jnp.where(c_i <= r_i, 1.0, 0.0).astype(MXU_DTYPE)
            keep = jnp.where(cut_needed, n_ties_kept.astype(f32), inf)

            def knock(tile, start, seen):
                tie = tile == thr
                rank = jnp.dot(prefix_op, jnp.where(tie, 1.0, 0.0).astype(MXU_DTYPE),
                               preferred_element_type=f32) + seen
                s_ref[pl.ds(start, 2 * kt), :] = jnp.where(jnp.logical_and(tie, rank > keep), -inf, tile)
                return rank[2 * kt - 1:2 * kt, :]
            sweep(knock, jnp.zeros((1, qb), f32))

        return thr

    unsettled = jnp.logical_and(searching, c_lo != topk)
    thr = lax.cond(jnp.max(unsettled.astype(jnp.int32)) > 0, settle, lambda t: t, thr)
    thr = jnp.where(searching, thr, smin)


    acc_ref[...] = jnp.zeros(acc_ref.shape, f32)
    slopes = [LOG2E * 2.0 ** (-8.0 * (h + 1) / N_HEADS) for h in range(N_HEADS)]
    q_posf = q_pos.astype(f32)

    def three_pieces(x):
        hi = x.astype(MXU_DTYPE).astype(f32)
        mid = (x - hi).astype(MXU_DTYPE).astype(f32)
        return [hi, mid, x - hi - mid]

    for h in range(N_HEADS):
        pieces = (three_pieces(jnp.full((1, qb), slopes[h] * CHUNK, f32))
                  + three_pieces(jnp.full((1, qb), slopes[h], f32))
                  + three_pieces(-slopes[h] * q_posf))
        pos_rows = jnp.concatenate(pieces + [jnp.zeros((POS_ROWS - len(pieces), qb), f32)], axis=0)
        rhs_ref[h, 0:HEAD_PAD, :] = qt_ref[0, 0, h * HEAD_PAD:(h + 1) * HEAD_PAD, :]
        rhs_ref[h, HEAD_PAD:HEAD_PAD + POS_ROWS, :] = pos_rows.astype(rhs_ref.dtype)
        rhs_ref[h, HEAD_PAD + POS_ROWS:, :] = jnp.zeros((LANES - POS_ROWS, qb), rhs_ref.dtype)

    def logits_head(t, slot, thr_t, diagonal=False):
        rows = tile_rows(t)
        mask_bias = jnp.where(s_ref[rows, :] >= thr_t, 0.0, -inf)
        pf = pf_ref[rows, :]
        if diagonal:
            twice_future = 2.0 * jnp.maximum(row_pos - lane_pos, 0).astype(f32)

        def head(h):
            lhs = jnp.concatenate([k_ref[0, rows, (h // 2) * LANES:(h // 2 + 1) * LANES], pf], axis=1)
            logit = jnp.dot(lhs, rhs_ref[h], preferred_element_type=f32) + mask_bias
            if diagonal:
                logit = logit - slopes[h] * twice_future
            lg_ref[h, slot * kt:(slot + 1) * kt, :] = logit
            return jnp.max(logit, axis=0, keepdims=True)
        return head

    def softmax_head(t, slot, m_old, tile_max, h):
        m_new = jnp.maximum(m_old, tile_max)
        m_safe = jnp.where(m_new == -inf, 0.0, m_new)
        alpha = jnp.exp2(m_old - m_safe)
        p = jnp.exp2(lg_ref[h, slot * kt:(slot + 1) * kt, :] - m_safe).astype(MXU_DTYPE)
        hrows = slice(h * V_PAD, (h + 1) * V_PAD)
        pv = jnp.dot(vt_ref[0, t, hrows, :], p, preferred_element_type=f32)
        acc_ref[hrows, :] = alpha * acc_ref[hrows, :] + pv
        return m_new

    def rows_of(x):
        return [x[h:h + 1, :] for h in range(N_HEADS)]

    def overlapped(t_soft, slot_soft, m_all, tile_max, t_next, slot_next, thr_next):
        head = logits_head(t_next, slot_next, thr_next)
        m_rows, max_rows = [], []
        max_rows += [head(h) for h in range(LOGITS_AHEAD)]
        for h, (m_h, max_h) in enumerate(zip(rows_of(m_all), rows_of(tile_max))):
            if h + LOGITS_AHEAD < N_HEADS:
                max_rows.append(head(h + LOGITS_AHEAD))
            m_rows.append(softmax_head(t_soft, slot_soft, m_h, max_h, h))
        return jnp.concatenate(m_rows, axis=0), jnp.concatenate(max_rows, axis=0)

    def tile_pair(j, carry):
        m_all, max_a = carry
        t0 = 2 * j
        t1 = t0 + 1
        m_all, max_b = overlapped(t0, 0, m_all, max_a, t1, 1, jnp.where(t1 < i, thr, inf))
        m_all, max_a = overlapped(t1, 1, m_all, max_b, jnp.minimum(t0 + 2, i), 0, thr)
        return m_all, max_a

    first = logits_head(0, 0, thr)
    m_all, _ = lax.fori_loop(
        0, (i + 1) >> 1, tile_pair,
        (jnp.full((N_HEADS, qb), -inf), jnp.concatenate([first(h) for h in range(N_HEADS)], axis=0)))
    last = logits_head(i, 0, thr, diagonal=True)
    last_max = [last(h) for h in range(N_HEADS)]
    for h, m_h in enumerate(rows_of(m_all)):
        softmax_head(i, 0, m_h, last_max[h], h)


    out = [acc_ref[h * V_PAD:h * V_PAD + HEAD_DIM, :] / acc_ref[h * V_PAD + HEAD_DIM:h * V_PAD + HEAD_DIM + 1, :]
           for h in range(N_HEADS)]
    o_ref[0] = jnp.concatenate(out, axis=0).T.astype(o_ref.dtype)


def _dsa(qt, qit, wit, k, kidx, vt, pf):
    b, ns, _, qb = qt.shape
    s = k.shape[1]
    topk = min(TOPK_MAX, s // 4)
    batch_spec = lambda shape: pl.BlockSpec((1,) + shape[1:], lambda bi, i: (bi,) + (0,) * (len(shape) - 1),
                                            pipeline_mode=pl.Buffered(1))
    tile_spec = lambda r: pl.BlockSpec((1, 1, r, qb), lambda bi, i: (bi, i, 0, 0))
    return pl.pallas_call(
        functools.partial(_dsa_kernel, topk),
        grid=(b, ns),
        in_specs=[
            tile_spec(qt.shape[2]), tile_spec(qit.shape[2]), tile_spec(wit.shape[2]),
            batch_spec(k.shape), batch_spec(kidx.shape), batch_spec(vt.shape), _resident(pf.shape),
        ],
        out_specs=pl.BlockSpec((1, qb, D_ATTN), lambda bi, i: (bi, i, 0)),
        out_shape=jax.ShapeDtypeStruct((b, s, D_ATTN), MXU_DTYPE),
        scratch_shapes=[
            pltpu.VMEM((s + SEQ_TILE, qb), jnp.float32),
            pltpu.VMEM((N_HEADS * V_PAD, qb), jnp.float32),
            pltpu.VMEM((N_HEADS, 2 * LANES, qb), MXU_DTYPE),
            pltpu.VMEM((N_HEADS, 2 * SEQ_TILE, qb), jnp.float32),
        ],
        compiler_params=pltpu.CompilerParams(
            dimension_semantics=("arbitrary", "arbitrary"), vmem_limit_bytes=VMEM_LIMIT_BYTES),
        name="dsa",
    )(qt, qit, wit, k, kidx, vt, pf)


def _tail_kernel(alpha, attn_ref, conv_ref, x1_ref, p_ref, woa_ref, woc_ref, g2_ref, b2_ref,
                 wg_ref, wu_ref, wd_ref, g3_ref, b3_ref, wgate_ref, wproj_ref, o_ref, act_ref):
    f32 = jnp.float32
    mix = jnp.dot(attn_ref[...], woa_ref[...], preferred_element_type=f32) \
        + jnp.dot(conv_ref[...], woc_ref[...], preferred_element_type=f32)
    x2 = _layer_norm(alpha * x1_ref[...] + mix, g2_ref[...], b2_ref[...])
    ffn = _swiglu(x2.astype(MXU_DTYPE), wg_ref, wu_ref, wd_ref, act_ref)
    x3 =_layer_norm(alpha * x2 + 0.5 * ffn, g3_ref[...], b3_ref[...])
    gate = jax.nn.sigmoid(jnp.dot(x3.astype(MXU_DTYPE), wgate_ref[...], preferred_element_type=f32))
    proj = jnp.dot(p_ref[...].astype(MXU_DTYPE), wproj_ref[...], preferred_element_type=f32)
    o_ref[...] = x3 + gate * proj


def _tail(attn, conv, x1, p, woa, woc, g2, b2, wg3, wu3, wd3, g3, b3, wgate, wproj, alpha):
    n, d = x1.shape
    tm = min(ROW_TILE, n)
    row = lambda w: pl.BlockSpec((tm, w), lambda i: (i, 0))
    weights = (woa, woc, g2, b2, wg3, wu3, wd3, g3, b3, wgate, wproj)
    return pl.pallas_call(
        functools.partial(_tail_kernel, alpha),
        grid=(n // tm,),
        in_specs=[row(attn.shape[1]), row(conv.shape[1]), row(d), row(p.shape[1])]
        + [_resident(w.shape) for w in weights],
        out_specs=row(d),
        out_shape=jax.ShapeDtypeStruct((n, d), jnp.float32),
        scratch_shapes=[pltpu.VMEM((tm, wd3.shape[0]), MXU_DTYPE)],
        compiler_params=pltpu.CompilerParams(
            dimension_semantics=("arbitrary",), vmem_limit_bytes=VMEM_LIMIT_BYTES),
        name="tail",
    )(attn, conv, x1, p, *weights)


def _ffn_weights(wg, wu, wd):
    assert wg.shape[1] % FF_CHUNK == 0
    return wg.astype(MXU_DTYPE), wu.astype(MXU_DTYPE), wd.astype(MXU_DTYPE)


def _key_position_features(s):
    pos = jnp.arange(s, dtype=jnp.int32)[:, None]
    col = jnp.arange(LANES, dtype=jnp.int32)[None, :]
    feat = jnp.where(col < 3, pos // CHUNK, jnp.where(col < 6, pos % CHUNK, jnp.where(col < 9, 1, 0)))
    return feat.astype(MXU_DTYPE)


def kernel(x, p, ln1_g, ln1_b, ffn1_wg, ffn1_wu, ffn1_wd, w_in, conv_w, w_out, ln2_g, ln2_b,
           ffn2_wg, ffn2_wu, ffn2_wd, ln3_g, ln3_b, ple_gate_w, ple_proj_w):
    b, s, d = x.shape
    depth = p.shape[0]
    alpha = (2 * depth) ** 0.25
    assert d == D_MODEL and s % SEQ_TILE == 0 and (b * s) % min(ROW_TILE, b * s) == 0
    o_q, o_k, o_v, o_qi = 0, D_ATTN, 2 * D_ATTN, 3 * D_ATTN
    o_ki = o_qi + N_IDX_HEADS * IDX_DIM
    o_wi = o_ki + IDX_DIM
    o_conv = o_wi + N_IDX_HEADS
    row2 = lambda v: v.reshape(1, -1)

    for i in range(depth):
        wi = w_in[i]
        x1 = _ffn_ln(x.reshape(b * s, d), *_ffn_weights(ffn1_wg[i], ffn1_wu[i], ffn1_wd[i]),
                     row2(ln1_g[i]), row2(ln1_b[i]), alpha)
        wk = wi[:, o_k:o_v].astype(MXU_DTYPE)
        wki = jnp.pad(wi[:, o_ki:o_wi], ((0, 0), (0, LANES - IDX_DIM))).astype(MXU_DTYPE)
        wqt = (wi[:, o_q:o_k] * (LOG2E * HEAD_DIM ** -0.5)).T.astype(MXU_DTYPE)
        wvt = wi[:, o_v:o_qi].T.astype(MXU_DTYPE)
        wqit = wi[:, o_qi:o_ki].T.astype(MXU_DTYPE)
        wwit = jnp.pad(wi[:, o_wi:o_conv].T, ((0, LANES - N_IDX_HEADS), (0, 0))).astype(MXU_DTYPE)
        wc = wi[:, o_conv:].astype(MXU_DTYPE)
        k, kidx, qt, vt, qit, wit, conv = _in_proj(
            x1.reshape(b, s, d), wk, wki, wqt, wvt, wqit, wwit, wc, conv_w[i])
        attn = _dsa(qt, qit, wit, k, kidx, vt, _key_position_features(s))
        wo = w_out[i].astype(MXU_DTYPE)
        x = _tail(attn.reshape(b * s, D_ATTN), conv.reshape(b * s, D_CONV), x1, p[i].reshape(b * s, -1),
                  wo[:D_ATTN], wo[D_ATTN:], row2(ln2_g[i]), row2(ln2_b[i]),
                  *_ffn_weights(ffn2_wg[i], ffn2_wu[i], ffn2_wd[i]),
                  row2(ln3_g[i]), row2(ln3_b[i]),
                  ple_gate_w[i].astype(MXU_DTYPE), ple_proj_w[i].astype(MXU_DTYPE), alpha).reshape(b, s, d)
    return x
```

```python
import functools

import jax
import jax.numpy as jnp
from jax import lax
from jax.experimental import pallas as pl
from jax.experimental.pallas import tpu as pltpu

D_MODEL = 1024
CHUNK = 64
HEAD_DIM = 64
N_HEADS = 8
D_ATTN = N_HEADS * HEAD_DIM
N_IDX_HEADS = 8
IDX_DIM = 64
TOPK_MAX = 256
D_CONV = D_MODEL - D_ATTN
LN_EPS = 1e-5
IDX_SCALE = (N_IDX_HEADS ** -0.5) * (IDX_DIM ** -0.5)
LOG2E = 1.4426950408889634

MXU_DTYPE = jnp.bfloat16

LANES = 128
FF_CHUNK = 256
ROW_TILE = 512
SEQ_TILE = 256
HEAD_PAD = 2 * HEAD_DIM
V_PAD = HEAD_DIM + 16
POS_ROWS = 16
LOGITS_AHEAD = 4
REDUCE_CHAINS = 4
BISECT_ROUNDS = 14
VMEM_LIMIT_BYTES = 56 * 1024 * 1024


def _layer_norm(y, g, b):
    mu = jnp.mean(y, axis=-1, keepdims=True)
    yc = y - mu
    var = jnp.mean(yc * yc, axis=-1, keepdims=True)
    return yc * lax.rsqrt(var + LN_EPS) * g + b


def _swiglu(xb, wg_ref, wu_ref, wd_ref, act_ref):
    for c in range(wg_ref.shape[1] // FF_CHUNK):
        cols = slice(c * FF_CHUNK, (c + 1) * FF_CHUNK)
        g = jnp.dot(xb, wg_ref[:, cols], preferred_element_type=jnp.float32)
        u = jnp.dot(xb, wu_ref[:, cols], preferred_element_type=jnp.float32)
        act_ref[:, cols] = ((g * jax.nn.sigmoid(g)) * u).astype(act_ref.dtype)
    return jnp.dot(act_ref[...], wd_ref[...], preferred_element_type=jnp.float32)


def _ffn_ln_kernel(alpha, x_ref, wg_ref, wu_ref, wd_ref, g_ref, b_ref, o_ref, act_ref):
    x = x_ref[...]
    ffn = _swiglu(x.astype(MXU_DTYPE), wg_ref, wu_ref, wd_ref, act_ref)
    o_ref[...] = _layer_norm(alpha * x + 0.5 * ffn, g_ref[...], b_ref[...])


def _resident(shape):
    nd = len(shape)
    return pl.BlockSpec(shape, lambda *_: (0,) * nd, pipeline_mode=pl.Buffered(1))


def _ffn_ln(x2d, wg3, wu3, wd3, g, b, alpha):
    n, d = x2d.shape
    tm = min(ROW_TILE, n)
    return pl.pallas_call(
        functools.partial(_ffn_ln_kernel, alpha),
        grid=(n // tm,),
        in_specs=[
            pl.BlockSpec((tm, d), lambda i: (i, 0)),
            _resident(wg3.shape), _resident(wu3.shape), _resident(wd3.shape),
            _resident(g.shape), _resident(b.shape),
        ],
        out_specs=pl.BlockSpec((tm, d), lambda i: (i, 0)),
        out_shape=jax.ShapeDtypeStruct((n, d), jnp.float32),
        scratch_shapes=[pltpu.VMEM((tm, wd3.shape[0]), MXU_DTYPE)],
        compiler_params=pltpu.CompilerParams(
            dimension_semantics=("arbitrary",), vmem_limit_bytes=VMEM_LIMIT_BYTES),
        name="ffn_ln",
    )(x2d, wg3, wu3, wd3, g, b)


def _in_proj_kernel(x_ref, wk_ref, wki_ref, wqt_ref, wvt_ref, wqit_ref, wwit_ref, wc_ref, cw_ref,
                    k_ref, kidx_ref, qt_ref, vt_ref, qit_ref, wit_ref, conv_ref, zbuf_ref):
    j = pl.program_id(1)
    tm = x_ref.shape[1]
    xb = x_ref[0].astype(MXU_DTYPE)
    nt = (((1,), (1,)), ((), ()))

    k_ref[0] = jnp.dot(xb, wk_ref[...], preferred_element_type=jnp.float32).astype(k_ref.dtype)
    kidx_ref[0] = jnp.dot(xb, wki_ref[...], preferred_element_type=jnp.float32).astype(kidx_ref.dtype)
    def padded_heads(yt, upper_half_of_head):
        zero = jnp.zeros((HEAD_DIM, tm), jnp.float32)
        groups = []
        for h in range(yt.shape[0] // HEAD_DIM):
            rows = yt[h * HEAD_DIM:(h + 1) * HEAD_DIM]
            groups += [zero, rows] if upper_half_of_head(h) else [rows, zero]
        return jnp.concatenate(groups, axis=0)

    qt = lax.dot_general(wqt_ref[...], xb, nt, preferred_element_type=jnp.float32)
    qt_ref[0, 0] = padded_heads(qt, lambda h: h % 2 == 1).astype(qt_ref.dtype)
    vt =lax.dot_general(wvt_ref[...], xb, nt, preferred_element_type=jnp.float32)
    ones = jnp.ones((V_PAD - HEAD_DIM, tm), jnp.float32)
    vt_ref[0, 0] = jnp.concatenate(
        [piece for h in range(N_HEADS) for piece in (vt[h * HEAD_DIM:(h + 1) * HEAD_DIM], ones)],
        axis=0).astype(vt_ref.dtype)
    qit = lax.dot_general(wqit_ref[...], xb, nt, preferred_element_type=jnp.float32)
    qit_ref[0, 0] = padded_heads(qit, lambda h: False).astype(qit_ref.dtype)
    wit = lax.dot_general(wwit_ref[...], xb, nt, preferred_element_type=jnp.float32)
    wit_ref[0, 0] = wit[0:N_IDX_HEADS, :] * IDX_SCALE

    h = jnp.dot(xb, wc_ref[...], preferred_element_type=jnp.float32)
    bg = h[:, 0:D_CONV]
    z = h[:, D_CONV:2 * D_CONV] * h[:, 2 * D_CONV:3 * D_CONV]

    @pl.when(j == 0)
    def _():
        zbuf_ref[0:8, :] = jnp.zeros((8, D_CONV), jnp.float32)

    zbuf_ref[8:8 + tm, :] = z
    z1 = zbuf_ref[7:7 + tm, :]
    z2 = zbuf_ref[6:6 + tm, :]
    y = (cw_ref[0:1, :] * z2 + cw_ref[1:2, :] * z1) + cw_ref[2:3, :] * z
    conv_ref[0] = (bg * y).astype(conv_ref.dtype)
    zbuf_ref[0:8, :] = z[tm - 8:tm, :]


def _in_proj(x1, wk, wki, wqt, wvt, wqit, wwit, wc, cw):
    b, s, d = x1.shape
    tm = SEQ_TILE
    ns = s // tm
    f32 = jnp.float32
    out_shape = (
        jax.ShapeDtypeStruct((b, s, D_ATTN), MXU_DTYPE),
        jax.ShapeDtypeStruct((b, s, LANES), MXU_DTYPE),
        jax.ShapeDtypeStruct((b, ns, N_HEADS * HEAD_PAD, tm), MXU_DTYPE),
        jax.ShapeDtypeStruct((b, ns, N_HEADS * V_PAD, tm), MXU_DTYPE),
        jax.ShapeDtypeStruct((b, ns, N_IDX_HEADS * HEAD_PAD, tm), MXU_DTYPE),
        jax.ShapeDtypeStruct((b, ns, N_IDX_HEADS, tm), f32),
        jax.ShapeDtypeStruct((b, s, D_CONV), MXU_DTYPE),
    )
    row_spec = lambda w: pl.BlockSpec((1, tm, w), lambda bi, j: (bi, j, 0))
    tile_spec = lambda r: pl.BlockSpec((1, 1, r, tm), lambda bi, j: (bi, j, 0, 0))
    return pl.pallas_call(
        _in_proj_kernel,
        grid=(b, ns),
        in_specs=[
            pl.BlockSpec((1, tm, d), lambda bi, j: (bi, j, 0)),
            _resident(wk.shape), _resident(wki.shape), _resident(wqt.shape), _resident(wvt.shape),
            _resident(wqit.shape), _resident(wwit.shape), _resident(wc.shape), _resident(cw.shape),
        ],
        out_specs=(
            row_spec(D_ATTN), row_spec(LANES), tile_spec(N_HEADS * HEAD_PAD), tile_spec(N_HEADS * V_PAD),
            tile_spec(N_IDX_HEADS * HEAD_PAD), tile_spec(N_IDX_HEADS), row_spec(D_CONV),
        ),
        out_shape=out_shape,
        scratch_shapes=[pltpu.VMEM((tm + 8, D_CONV), f32)],
        compiler_params=pltpu.CompilerParams(
            dimension_semantics=("arbitrary", "arbitrary"), vmem_limit_bytes=VMEM_LIMIT_BYTES),
        name="in_proj",
    )(x1, wk, wki, wqt, wvt, wqit, wwit, wc, cw)


def _col_reduce(op, x):
    r, c = x.shape
    chains = [None] * REDUCE_CHAINS
    for g in range(r // 8):
        part = x[g * 8:(g + 1) * 8, :]
        k = g % REDUCE_CHAINS
        chains[k] = part if chains[k] is None else op(chains[k], part)
    out = chains[0]
    for extra in chains[1:]:
        if extra is not None:
            out = op(out, extra)
    return out


def _col_count(mask):
    return _col_reduce(jnp.add, jnp.where(mask, 1, 0).astype(jnp.int32))


def _col_min(x):
    return _col_reduce(jnp.minimum, x)


def _dsa_kernel(topk, qt_ref, qit_ref, wit_ref, k_ref, kidx_ref, vt_ref, pf_ref, o_ref,
                s_ref, acc_ref, rhs_ref, lg_ref):
    i = pl.program_id(1)
    kt = SEQ_TILE
    qb = qt_ref.shape[3]
    f32 = jnp.float32
    inf = jnp.float32(jnp.inf)
    n_tiles = i + 1

    chunk_shift = CHUNK.bit_length() - 1
    lane_pos = lax.broadcasted_iota(jnp.int32, (1, qb), 1)
    row_pos = lax.broadcasted_iota(jnp.int32, (kt, 1), 0)
    q_pos = i * qb + lane_pos
    n_adm = ((q_pos >> chunk_shift) + 1) * CHUNK

    def tile_rows(t):
        return pl.ds(pl.multiple_of(t * kt, kt), kt)

    def score_rows(rows, n_rows):
        kx = kidx_ref[0, rows, :]
        s = jnp.zeros((n_rows, qb), f32)
        for h in range(N_IDX_HEADS):
            rel = jnp.dot(kx, qit_ref[0, 0, h * HEAD_PAD:(h + 1) * HEAD_PAD, :],
                          preferred_element_type=f32)
            s = s + jnp.maximum(rel, 0.0) * wit_ref[0, 0, h:h + 1, :]
        return s

    def score_tile(t):
        return score_rows(tile_rows(t), kt)

    def phase_a(j, carry):
        smax, smin = carry
        rows = pl.ds(pl.multiple_of(j * (2 * kt), 2 * kt), 2 * kt)
        s = score_rows(rows, 2 * kt)
        s_ref[rows, :] = s
        second_counted = 2 * j + 1 < i
        for half, counted in ((s[0:kt], None), (s[kt:2 * kt], second_counted)):
            tmax = jnp.max(half, axis=0, keepdims=True)
            tmin = jnp.min(half, axis=0, keepdims=True)
            if counted is not None:
                tmax = jnp.where(counted, tmax, -inf)
                tmin = jnp.where(counted, tmin, inf)
            smax = jnp.maximum(smax, tmax)
            smin = jnp.minimum(smin, tmin)
        return smax, smin

    smax, smin = lax.fori_loop(0, (i + 1) >> 1, phase_a,
                               (jnp.full((1, qb), -inf), jnp.full((1, qb), inf)))
    s = score_tile(i)
    adm = (row_pos >> chunk_shift) <= (lane_pos >> chunk_shift)
    s_ref[tile_rows(i), :] = jnp.where(adm, s, -inf)
    smax = jnp.maximum(smax, jnp.max(jnp.where(adm, s, -inf), axis=0, keepdims=True))
    smin = jnp.minimum(smin, jnp.min(jnp.where(adm, s, inf), axis=0, keepdims=True))

    s_ref[tile_rows(n_tiles), :] = jnp.full((kt, qb), -inf)

    def sweep(fn, init):
        def body(j, carry):
            start = pl.multiple_of(j * (2 * kt), 2 * kt)
            return fn(s_ref[pl.ds(start, 2 * kt), :], start, carry)
        return lax.fori_loop(0, (n_tiles + 1) >> 1, body, init)

    def count_ge(p):
        c8 = sweep(lambda tile, start, c: c + _col_count(tile >= p), jnp.zeros((8, qb), jnp.int32))
        return jnp.sum(c8, axis=0, keepdims=True)

    searching = n_adm > topk

    def bisect(_, carry):
        lo, hi, c_lo = carry
        p = lo * 0.5 + hi * 0.5
        c = count_ge(p)
        up = c >= topk
        return jnp.where(up, p, lo), jnp.where(up, hi, p), jnp.where(up, c, c_lo)

    lo, _, c_lo = lax.fori_loop(0, BISECT_ROUNDS, bisect, (smin, smax, n_adm))
    thr = jnp.min(sweep(lambda tile, start, c: jnp.minimum(c, _col_min(jnp.where(tile >= lo, tile, inf))),
                        jnp.full((8, qb), inf)), axis=0, keepdims=True)

    def above(thr):
        def fn(tile, start, carry):
            gt8, nx8 = carry
            gt = tile > thr
            return gt8 + _col_count(gt), jnp.minimum(nx8, _col_min(jnp.where(gt, tile, inf)))
        gt8, nx8 = sweep(fn, (jnp.zeros((8, qb), jnp.int32), jnp.full((8, qb), inf)))
        return jnp.sum(gt8, axis=0, keepdims=True), jnp.min(nx8, axis=0, keepdims=True)

    def settle(thr):
        def peel_cond(carry):
            _, _, n_gt, _ = carry
            return jnp.max(n_gt) >= topk

        def peel_body(carry):
            thr, n_ge, n_gt, nxt = carry
            step = n_gt >= topk
            thr = jnp.where(step, nxt, thr)
            return (thr, jnp.where(step, n_gt, n_ge)) + above(thr)

        thr, n_ge, n_gt, _ = lax.while_loop(peel_cond, peel_body, (thr, c_lo) + above(thr))
        n_ties_kept = topk - n_gt
        cut_needed = jnp.logical_and(searching, n_ge > topk)

        @pl.when(jnp.max(cut_needed.astype(jnp.int32)) > 0)
        def _():
            r_i = lax.broadcasted_iota(jnp.int32, (2 * kt, 2 * kt), 0)
            c_i = lax.broadcasted_iota(jnp.int32, (2 * kt, 2 * kt), 1)
            prefix_op = jnp.where(c_i <= r_i, 1.0, 0.0).astype(MXU_DTYPE)
            keep = jnp.where(cut_needed, n_ties_kept.astype(f32), inf)

            def knock(tile, start, seen):
                tie = tile == thr
                rank = jnp.dot(prefix_op, jnp.where(tie, 1.0, 0.0).astype(MXU_DTYPE),
                               preferred_element_type=f32) + seen
                s_ref[pl.ds(start, 2 * kt), :] = jnp.where(jnp.logical_and(tie, rank > keep), -inf, tile)
                return rank[2 * kt - 1:2 * kt, :]
            sweep(knock, jnp.zeros((1, qb), f32))

        return thr

    unsettled = jnp.logical_and(searching, c_lo != topk)
    thr = lax.cond(jnp.max(unsettled.astype(jnp.int32)) > 0, settle, lambda t: t, thr)
    thr = jnp.where(searching, thr, smin)


    acc_ref[...] = jnp.zeros(acc_ref.shape, f32)
    slopes = [LOG2E * 2.0 ** (-8.0 * (h + 1) / N_HEADS) for h in range(N_HEADS)]
    q_posf = q_pos.astype(f32)

    def three_pieces(x):
        hi = x.astype(MXU_DTYPE).astype(f32)
        mid = (x - hi).astype(MXU_DTYPE).astype(f32)
        return [hi, mid, x - hi - mid]

    for h in range(N_HEADS):
        pieces = (three_pieces(jnp.full((1, qb), slopes[h] * CHUNK, f32))
                  + three_pieces(jnp.full((1, qb), slopes[h], f32))
                  + three_pieces(-slopes[h] * q_posf))
        pos_rows = jnp.concatenate(pieces + [jnp.zeros((POS_ROWS - len(pieces), qb), f32)], axis=0)
        rhs_ref[h, 0:HEAD_PAD, :] = qt_ref[0, 0, h * HEAD_PAD:(h + 1) * HEAD_PAD, :]
        rhs_ref[h, HEAD_PAD:HEAD_PAD + POS_ROWS, :] = pos_rows.astype(rhs_ref.dtype)
        rhs_ref[h, HEAD_PAD + POS_ROWS:, :] = jnp.zeros((LANES - POS_ROWS, qb), rhs_ref.dtype)

    def logits_head(t, slot, thr_t, diagonal=False):
        rows = tile_rows(t)
        mask_bias = jnp.where(s_ref[rows, :] >= thr_t, 0.0, -inf)
        pf = pf_ref[rows, :]
        if diagonal:
            twice_future = 2.0 * jnp.maximum(row_pos - lane_pos, 0).astype(f32)

        def head(h):
            lhs = jnp.concatenate([k_ref[0, rows, (h // 2) * LANES:(h // 2 + 1) * LANES], pf], axis=1)
            logit = jnp.dot(lhs, rhs_ref[h], preferred_element_type=f32) + mask_bias
            if diagonal:
                logit = logit - slopes[h] * twice_future
            lg_ref[h, slot * kt:(slot + 1) * kt, :] = logit
            return jnp.max(logit, axis=0, keepdims=True)
        return head

    def softmax_head(t, slot, m_old, tile_max, h):
        m_new = jnp.maximum(m_old, tile_max)
        m_safe = jnp.where(m_new == -inf, 0.0, m_new)
        alpha = jnp.exp2(m_old - m_safe)
        p = jnp.exp2(lg_ref[h, slot * kt:(slot + 1) * kt, :] - m_safe).astype(MXU_DTYPE)
        hrows = slice(h * V_PAD, (h + 1) * V_PAD)
        pv = jnp.dot(vt_ref[0, t, hrows, :], p, preferred_element_type=f32)
        acc_ref[hrows, :] = alpha * acc_ref[hrows, :] + pv
        return m_new

    def rows_of(x):
        return [x[h:h + 1, :] for h in range(N_HEADS)]

    def overlapped(t_soft, slot_soft, m_all, tile_max, t_next, slot_next, thr_next):
        head = logits_head(t_next, slot_next, thr_next)
        m_rows, max_rows = [], []
        max_rows += [head(h) for h in range(LOGITS_AHEAD)]
        for h, (m_h, max_h) in enumerate(zip(rows_of(m_all), rows_of(tile_max))):
            if h + LOGITS_AHEAD < N_HEADS:
                max_rows.append(head(h + LOGITS_AHEAD))
            m_rows.append(softmax_head(t_soft, slot_soft, m_h, max_h, h))
        return jnp.concatenate(m_rows, axis=0), jnp.concatenate(max_rows, axis=0)

    def tile_pair(j, carry):
        m_all, max_a = carry
        t0 = 2 * j
        t1 = t0 + 1
        m_all, max_b = overlapped(t0, 0, m_all, max_a, t1, 1, jnp.where(t1 < i, thr, inf))
        m_all, max_a = overlapped(t1, 1, m_all, max_b, jnp.minimum(t0 + 2, i), 0, thr)
        return m_all, max_a

    first = logits_head(0, 0, thr)
    m_all, _ = lax.fori_loop(
        0, (i + 1) >> 1, tile_pair,
        (jnp.full((N_HEADS, qb), -inf), jnp.concatenate([first(h) for h in range(N_HEADS)], axis=0)))
    last = logits_head(i, 0, thr, diagonal=True)
    last_max = [last(h) for h in range(N_HEADS)]
    for h, m_h in enumerate(rows_of(m_all)):
        softmax_head(i, 0, m_h, last_max[h], h)


    out = [acc_ref[h * V_PAD:h * V_PAD + HEAD_DIM, :] / acc_ref[h * V_PAD + HEAD_DIM:h * V_PAD + HEAD_DIM + 1, :]
           for h in range(N_HEADS)]
    o_ref[0] = jnp.concatenate(out, axis=0).T.astype(o_ref.dtype)


def _dsa(qt, qit, wit, k, kidx, vt, pf):
    b, ns, _, qb = qt.shape
    s = k.shape[1]
    topk = min(TOPK_MAX, s // 4)
    batch_spec = lambda shape: pl.BlockSpec((1,) + shape[1:], lambda bi, i: (bi,) + (0,) * (len(shape) - 1),
                                            pipeline_mode=pl.Buffered(1))
    tile_spec = lambda r: pl.BlockSpec((1, 1, r, qb), lambda bi, i: (bi, i, 0, 0))
    return pl.pallas_call(
        functools.partial(_dsa_kernel, topk),
        grid=(b, ns),
        in_specs=[
            tile_spec(qt.shape[2]), tile_spec(qit.shape[2]), tile_spec(wit.shape[2]),
            batch_spec(k.shape), batch_spec(kidx.shape), batch_spec(vt.shape), _resident(pf.shape),
        ],
        out_specs=pl.BlockSpec((1, qb, D_ATTN), lambda bi, i: (bi, i, 0)),
        out_shape=jax.ShapeDtypeStruct((b, s, D_ATTN), MXU_DTYPE),
        scratch_shapes=[
            pltpu.VMEM((s + SEQ_TILE, qb), jnp.float32),
            pltpu.VMEM((N_HEADS * V_PAD, qb), jnp.float32),
            pltpu.VMEM((N_HEADS, 2 * LANES, qb), MXU_DTYPE),
            pltpu.VMEM((N_HEADS, 2 * SEQ_TILE, qb), jnp.float32),
        ],
        compiler_params=pltpu.CompilerParams(
            dimension_semantics=("arbitrary", "arbitrary"), vmem_limit_bytes=VMEM_LIMIT_BYTES),
        name="dsa",
    )(qt, qit, wit, k, kidx, vt, pf)


def _tail_kernel(alpha, attn_ref, conv_ref, x1_ref, p_ref, woa_ref, woc_ref, g2_ref, b2_ref,
                 wg_ref, wu_ref, wd_ref, g3_ref, b3_ref, wgate_ref, wproj_ref, o_ref, act_ref):
    f32 = jnp.float32
    mix = jnp.dot(attn_ref[...], woa_ref[...], preferred_element_type=f32) \
        + jnp.dot(conv_ref[...], woc_ref[...], preferred_element_type=f32)
    x2 = _layer_norm(alpha * x1_ref[...] + mix, g2_ref[...], b2_ref[...])
    ffn = _swiglu(x2.astype(MXU_DTYPE), wg_ref, wu_ref, wd_ref, act_ref)
    x3 =_layer_norm(alpha * x2 + 0.5 * ffn, g3_ref[...], b3_ref[...])
    gate = jax.nn.sigmoid(jnp.dot(x3.astype(MXU_DTYPE), wgate_ref[...], preferred_element_type=f32))
    proj = jnp.dot(p_ref[...].astype(MXU_DTYPE), wproj_ref[...], preferred_element_type=f32)
    o_ref[...] = x3 + gate * proj


def _tail(attn, conv, x1, p, woa, woc, g2, b2, wg3, wu3, wd3, g3, b3, wgate, wproj, alpha):
    n, d = x1.shape
    tm = min(ROW_TILE, n)
    row = lambda w: pl.BlockSpec((tm, w), lambda i: (i, 0))
    weights = (woa, woc, g2, b2, wg3, wu3, wd3, g3, b3, wgate, wproj)
    return pl.pallas_call(
        functools.partial(_tail_kernel, alpha),
        grid=(n // tm,),
        in_specs=[row(attn.shape[1]), row(conv.shape[1]), row(d), row(p.shape[1])]
        + [_resident(w.shape) for w in weights],
        out_specs=row(d),
        out_shape=jax.ShapeDtypeStruct((n, d), jnp.float32),
        scratch_shapes=[pltpu.VMEM((tm, wd3.shape[0]), MXU_DTYPE)],
        compiler_params=pltpu.CompilerParams(
            dimension_semantics=("arbitrary",), vmem_limit_bytes=VMEM_LIMIT_BYTES),
        name="tail",
    )(attn, conv, x1, p, *weights)


def _ffn_weights(wg, wu, wd):
    assert wg.shape[1] % FF_CHUNK == 0
    return wg.astype(MXU_DTYPE), wu.astype(MXU_DTYPE), wd.astype(MXU_DTYPE)


def _key_position_features(s):
    pos = jnp.arange(s, dtype=jnp.int32)[:, None]
    col = jnp.arange(LANES, dtype=jnp.int32)[None, :]
    feat = jnp.where(col < 3, pos // CHUNK, jnp.where(col < 6, pos % CHUNK, jnp.where(col < 9, 1, 0)))
    return feat.astype(MXU_DTYPE)


def kernel(x, p, ln1_g, ln1_b, ffn1_wg, ffn1_wu, ffn1_wd, w_in, conv_w, w_out, ln2_g, ln2_b,
           ffn2_wg, ffn2_wu, ffn2_wd, ln3_g, ln3_b, ple_gate_w, ple_proj_w):
    b, s, d = x.shape
    depth = p.shape[0]
    alpha = (2 * depth) ** 0.25
    assert d == D_MODEL and s % SEQ_TILE == 0 and (b * s) % min(ROW_TILE, b * s) == 0
    o_q, o_k, o_v, o_qi = 0, D_ATTN, 2 * D_ATTN, 3 * D_ATTN
    o_ki = o_qi + N_IDX_HEADS * IDX_DIM
    o_wi = o_ki + IDX_DIM
    o_conv = o_wi + N_IDX_HEADS
    row2 = lambda v: v.reshape(1, -1)

    for i in range(depth):
        wi = w_in[i]
        x1 = _ffn_ln(x.reshape(b * s, d), *_ffn_weights(ffn1_wg[i], ffn1_wu[i], ffn1_wd[i]),
                     row2(ln1_g[i]), row2(ln1_b[i]), alpha)
        wk = wi[:, o_k:o_v].astype(MXU_DTYPE)
        wki = jnp.pad(wi[:, o_ki:o_wi], ((0, 0), (0, LANES - IDX_DIM))).astype(MXU_DTYPE)
        wqt = (wi[:, o_q:o_k] * (LOG2E * HEAD_DIM ** -0.5)).T.astype(MXU_DTYPE)
        wvt = wi[:, o_v:o_qi].T.astype(MXU_DTYPE)
        wqit = wi[:, o_qi:o_ki].T.astype(MXU_DTYPE)
        wwit = jnp.pad(wi[:, o_wi:o_conv].T, ((0, LANES - N_IDX_HEADS), (0, 0))).astype(MXU_DTYPE)
        wc = wi[:, o_conv:].astype(MXU_DTYPE)
        k, kidx, qt, vt, qit, wit, conv = _in_proj(
            x1.reshape(b, s, d), wk, wki, wqt, wvt, wqit, wwit, wc, conv_w[i])
        attn = _dsa(qt, qit, wit, k, kidx, vt, _key_position_features(s))
        wo = w_out[i].astype(MXU_DTYPE)
        x = _tail(attn.reshape(b * s, D_ATTN), conv.reshape(b * s, D_CONV), x1, p[i].reshape(b * s, -1),
                  wo[:D_ATTN], wo[D_ATTN:], row2(ln2_g[i]), row2(ln2_b[i]),
                  *_ffn_weights(ffn2_wg[i], ffn2_wu[i], ffn2_wd[i]),
                  row2(ln3_g[i]), row2(ln3_b[i]),
                  ple_gate_w[i].astype(MXU_DTYPE), ple_proj_w[i].astype(MXU_DTYPE), alpha).reshape(b, s, d)
    return x
```
